```python
import jax, jax.numpy as jnp
from jax import lax
import numpy as np

D_MODEL = 1024
BATCH = 8
SEQ = 2048
DEPTH = 1

CHUNK = 64
CONV_WIDTH = D_MODEL
CONV_K = 31
LRU_WIDTH = D_MODEL
LRU_HEADS = 8
LRU_HEAD_DIM = LRU_WIDTH // LRU_HEADS
LRU_CONV_K = 4
LRU_C = 8.0
N_BRANCH = 2
N_EXPERTS = 32
TOP_K = 4
D_EXPERT = D_MODEL
SWIGLU_LIMIT = 7.0
SWIGLU_ALPHA = 1.702
MOE_BLOCK = 128
LN_EPS = 1e-5
DEEPNORM_ALPHA = (2.0 * DEPTH) ** 0.25
DEEPNORM_BETA = (8.0 * DEPTH) ** -0.25
IN_COLS = 2 * CONV_WIDTH + 2 * LRU_WIDTH + N_BRANCH * D_MODEL

kernel_name = 'hybrid_conv_rglru_moe_block'


def layer_norm(x, gain=None, bias=None):
    xf = x.astype(jnp.float32)
    mu = xf.mean(-1, keepdims=True)
    var = jnp.square(xf - mu).mean(-1, keepdims=True)
    y = (xf - mu) * lax.rsqrt(var + LN_EPS)
    if gain is not None:
        y = y * gain.astype(jnp.float32) + bias.astype(jnp.float32)
    return y.astype(x.dtype)


def causal_depthwise_conv(x, w, b):
    k = w.shape[0]
    xp = jnp.pad(x, ((0, 0), (k - 1, 0), (0, 0)))
    y = lax.conv_general_dilated(xp, w[:, None, :], window_strides=(1,), padding='VALID',
                                 dimension_numbers=('NWC', 'WIO', 'NWC'),
                                 feature_group_count=x.shape[-1])
    return y + b


def rg_lru(x, w_a, b_a, w_x, b_x, lam):
    bsz, s, _ = x.shape
    xh = x.reshape(bsz, s, LRU_HEADS, LRU_HEAD_DIM)
    gate_a = jax.nn.sigmoid(jnp.einsum('bshi,hij->bshj', xh, w_a).reshape(bsz, s, LRU_WIDTH) + b_a)
    gate_x = jax.nn.sigmoid(jnp.einsum('bshi,hij->bshj', xh, w_x).reshape(bsz, s, LRU_WIDTH) + b_x)
    log_a = (-LRU_C * gate_a.astype(jnp.float32)) * jax.nn.softplus(-lam.astype(jnp.float32))
    a = jnp.exp(log_a)
    mult = jnp.sqrt(-jnp.expm1(2.0 * log_a))
    is_first = (jnp.arange(s) == 0)[None, :, None]
    mult = jnp.where(is_first, 1.0, mult)
    u = mult * (gate_x * x).astype(jnp.float32)

    def combine(lhs, rhs):
        a1, b1 = lhs
        a2, b2 = rhs
        return a1 * a2, a2 * b1 + b2

    _, h = lax.associative_scan(combine, (a, u), axis=1)
    return h.astype(x.dtype)


def hybrid_mixer(u, w_in, b_in, w_cv_dw, b_cv_dw, ln_cv_g, ln_cv_b, w_cv_out,
                 w_lru_conv, b_lru_conv, w_lru_a, b_lru_a, w_lru_x, b_lru_x, lru_lambda,
                 w_lru_out, w_o, b_o):
    z = u @ w_in + b_in
    cv_in, lru_in, lru_gate, merge = jnp.split(
        z, [2 * CONV_WIDTH, 2 * CONV_WIDTH + LRU_WIDTH, 2 * CONV_WIDTH + 2 * LRU_WIDTH], axis=-1)
    a_val, a_gate = jnp.split(cv_in, 2, axis=-1)
    ya = a_val * jax.nn.sigmoid(a_gate)
    ya = causal_depthwise_conv(ya, w_cv_dw, b_cv_dw)
    ya = jax.nn.silu(layer_norm(ya, ln_cv_g, ln_cv_b))
    ya = ya @ w_cv_out
    yb = causal_depthwise_conv(lru_in, w_lru_conv, b_lru_conv)
    yb = rg_lru(yb, w_lru_a, b_lru_a, w_lru_x, b_lru_x, lru_lambda)
    yb = (yb * jax.nn.gelu(lru_gate)) @ w_lru_out
    g_a, g_b = jnp.split(jax.nn.sigmoid(merge), 2, axis=-1)
    return (g_a * ya + g_b * yb) @ w_o + b_o


def moe_ffn(u, w_router, b_router, w_up, b_up, w_down, b_down):
    bsz, s, d = u.shape
    t = bsz * s
    h = u.reshape(t, d)
    logits = (h @ w_router + b_router).astype(jnp.float32)
    top_logits, top_idx = lax.top_k(logits, TOP_K)
    top_w = jax.nn.softmax(top_logits, axis=-1).astype(u.dtype)
    n_assign = t * TOP_K
    flat_e = top_idx.reshape(n_assign)
    flat_tok = jnp.repeat(jnp.arange(t, dtype=jnp.int32), TOP_K)
    flat_w = top_w.reshape(n_assign)
    order = jnp.argsort(flat_e)
    sorted_e = flat_e[order]
    counts = jnp.bincount(flat_e, length=N_EXPERTS)
    padded = (counts + MOE_BLOCK - 1) // MOE_BLOCK * MOE_BLOCK
    start = jnp.cumsum(counts) - counts
    pend = jnp.cumsum(padded)
    pstart = pend - padded
    dest = pstart[sorted_e] + jnp.arange(n_assign, dtype=jnp.int32) - start[sorted_e]
    n_blocks = -(-(n_assign + N_EXPERTS * MOE_BLOCK) // MOE_BLOCK)
    n_rows = n_blocks * MOE_BLOCK
    row_tok = jnp.full((n_rows,), t, jnp.int32).at[dest].set(flat_tok[order])
    row_w = jnp.zeros((n_rows,), u.dtype).at[dest].set(flat_w[order])
    block_e = jnp.minimum(
        jnp.searchsorted(pend, jnp.arange(n_blocks) * MOE_BLOCK, side='right'), N_EXPERTS - 1)
    h_pad = jnp.concatenate([h, jnp.zeros((1, d), h.dtype)], axis=0)
    xb = h_pad[row_tok].reshape(n_blocks, MOE_BLOCK, d)

    def expert_block(args):
        xblk, e = args
        zz = xblk @ w_up[e] + b_up[e]
        z_glu = jnp.minimum(zz[:, ::2], SWIGLU_LIMIT)
        z_lin = jnp.clip(zz[:, 1::2], -SWIGLU_LIMIT, SWIGLU_LIMIT)
        act = z_glu * jax.nn.sigmoid(SWIGLU_ALPHA * z_glu) * (z_lin + 1.0)
        return act @ w_down[e] + b_down[e]

    yb = lax.map(expert_block, (xb, block_e)).reshape(n_rows, d)
    y = jnp.zeros((t + 1, d), u.dtype).at[row_tok].add(yb * row_w[:, None])
    return y[:t].reshape(bsz, s, d)


def setup_inputs(seed: int = 0) -> dict:
    key = jax.random.key(seed)
    ks = iter(jax.random.split(key, 40))
    f32 = jnp.float32
    L, D = DEPTH, D_MODEL

    def nrm(shape, scale):
        return jax.random.normal(next(ks), shape, f32) * scale

    x = nrm((BATCH, SEQ, D), 1.0)
    c = nrm((BATCH, D), 1.0)
    lam_u = jax.random.uniform(next(ks), (L, LRU_WIDTH), f32, 0.9, 0.999)
    sig = lam_u ** (1.0 / LRU_C)
    lru_lambda = jnp.log(sig) - jnp.log1p(-sig)
    return {
        'x': x,
        'c': c,
        'w_ada': nrm((L, D, 6 * D), D ** -0.5),
        'b_ada': nrm((L, 6 * D), 0.02),
        'w_in': nrm((L, D, IN_COLS), D ** -0.5),
        'b_in': nrm((L, IN_COLS), 0.02),
        'w_cv_dw': nrm((L, CONV_K, CONV_WIDTH), CONV_K ** -0.5),
        'b_cv_dw': nrm((L, CONV_WIDTH), 0.02),
        'ln_cv_g': 1.0 + nrm((L, CONV_WIDTH), 0.02),
        'ln_cv_b': nrm((L, CONV_WIDTH), 0.02),
        'w_cv_out': nrm((L, CONV_WIDTH, D), CONV_WIDTH ** -0.5 * DEEPNORM_BETA),
        'w_lru_conv': nrm((L, LRU_CONV_K, LRU_WIDTH), LRU_CONV_K ** -0.5),
        'b_lru_conv': nrm((L, LRU_WIDTH), 0.02),
        'w_lru_a': nrm((L, LRU_HEADS, LRU_HEAD_DIM, LRU_HEAD_DIM), LRU_HEAD_DIM ** -0.5),
        'b_lru_a': nrm((L, LRU_WIDTH), 0.02),
        'w_lru_x': nrm((L, LRU_HEADS, LRU_HEAD_DIM, LRU_HEAD_DIM), LRU_HEAD_DIM ** -0.5),
        'b_lru_x': nrm((L, LRU_WIDTH), 0.02),
        'lru_lambda': lru_lambda,
        'w_lru_out': nrm((L, LRU_WIDTH, D), LRU_WIDTH ** -0.5 * DEEPNORM_BETA),
        'w_o': nrm((L, D, D), D ** -0.5 * DEEPNORM_BETA),
        'b_o': nrm((L, D), 0.02),
        'ln1_g': 1.0 + nrm((L, D), 0.02),
        'ln1_b': nrm((L, D), 0.02),
        'w_router': nrm((L, D, N_EXPERTS), D ** -0.5),
        'b_router': nrm((L, N_EXPERTS), 0.01),
        'w_up': nrm((L, N_EXPERTS, D, 2 * D_EXPERT), D ** -0.5 * DEEPNORM_BETA),
        'b_up': nrm((L, N_EXPERTS, 2 * D_EXPERT), 0.02),
        'w_down': nrm((L, N_EXPERTS, D_EXPERT, D), D_EXPERT ** -0.5 * DEEPNORM_BETA),
        'b_down': nrm((L, N_EXPERTS, D), 0.02),
        'ln2_g': 1.0 + nrm((L, D), 0.02),
        'ln2_b': nrm((L, D), 0.02),
    }


def reference(x, c, w_ada, b_ada, w_in, b_in, w_cv_dw, b_cv_dw, ln_cv_g, ln_cv_b, w_cv_out,
              w_lru_conv, b_lru_conv, w_lru_a, b_lru_a, w_lru_x, b_lru_x, lru_lambda,
              w_lru_out, w_o, b_o, ln1_g, ln1_b, w_router, b_router, w_up, b_up,
              w_down, b_down, ln2_g, ln2_b):
    h = x
    for l in range(DEPTH):
        mod = jax.nn.silu(c) @ w_ada[l] + b_ada[l]
        sh1, sc1, g1, sh2, sc2, g2 = jnp.split(mod[:, None, :], 6, axis=-1)
        u = layer_norm(h) * (1.0 + sc1) + sh1
        mix = hybrid_mixer(u, w_in[l], b_in[l], w_cv_dw[l], b_cv_dw[l], ln_cv_g[l], ln_cv_b[l],
                           w_cv_out[l], w_lru_conv[l], b_lru_conv[l], w_lru_a[l], b_lru_a[l],
                           w_lru_x[l], b_lru_x[l], lru_lambda[l], w_lru_out[l], w_o[l], b_o[l])
        h = layer_norm(DEEPNORM_ALPHA * h + g1 * mix, ln1_g[l], ln1_b[l])
        u = layer_norm(h) * (1.0 + sc2) + sh2
        ffn = moe_ffn(u, w_router[l], b_router[l], w_up[l], b_up[l], w_down[l], b_down[l])
        h = layer_norm(DEEPNORM_ALPHA * h + g2 * ffn, ln2_g[l], ln2_b[l])
    return h
```

```python
import functools

import jax
import jax.numpy as jnp
from jax import lax
from jax.experimental import pallas as pl
from jax.experimental.pallas import tpu as pltpu

F32 = jnp.float32
BF16 = jnp.bfloat16
HIGHEST = lax.Precision.HIGHEST

D = 1024
BATCH = 8
SEQ = 2048
T = BATCH * SEQ
CONV_K = 31
LRU_CONV_K = 4
LRU_HEADS = 8
LRU_HEAD_DIM = D // LRU_HEADS
LRU_C = 8.0
N_EXPERTS = 32
TOP_K = 4
D_EXPERT = D
SWIGLU_LIMIT = 7.0
SWIGLU_ALPHA = 1.702
LN_EPS = 1e-5
DEEPNORM_ALPHA = 2.0 ** 0.25
IN_COLS = 6 * D

LANES = 128
SUBLANES = 8

TS = 64
ROWS = TS * BATCH
CV_HIST = (CONV_K - 1) * BATCH
LRU_HIST = (LRU_CONV_K - 1) * BATCH
CONV_CHUNK = 64

EBLK = 256
N_EBLK = (T * TOP_K + N_EXPERTS * (EBLK - 1) + EBLK - 1) // EBLK
N_EROWS = N_EBLK * EBLK
TOK_TILE = 256

ROUTE_W = LANES
NEG_BIG = -1e30


def _layer_norm(x):
    mu = jnp.mean(x, axis=-1, keepdims=True)
    xc = x - mu
    var = jnp.mean(xc * xc, axis=-1, keepdims=True)
    return xc * lax.rsqrt(var + LN_EPS)


def _per_batch(v, scale=None, shift=None):
    rows = v.shape[0]
    v3 = v.reshape(rows // BATCH, BATCH, D)
    if scale is not None:
        v3 = v3 * scale[None]
    if shift is not None:
        v3 = v3 + shift[None]
    return v3.reshape(rows, D)


def _bdot(a, w):
    return jnp.dot(a.astype(BF16), w, preferred_element_type=F32)


def _mod_kernel(c_ref, w_ref, b_ref, o_ref):
    c = c_ref[...]
    o_ref[...] = jnp.dot(c * jax.nn.sigmoid(c), w_ref[...], preferred_element_type=F32,
                         precision=HIGHEST) + b_ref[...]


def _modulation(c, w_ada, b_ada):
    tn = 1024
    return pl.pallas_call(
        _mod_kernel,
        out_shape=jax.ShapeDtypeStruct((BATCH, 6 * D), F32),
        grid=(6 * D // tn,),
        in_specs=[pl.BlockSpec((BATCH, D), lambda j: (0, 0)),
                  pl.BlockSpec((D, tn), lambda j: (0, j)),
                  pl.BlockSpec((1, tn), lambda j: (0, j))],
        out_specs=pl.BlockSpec((BATCH, tn), lambda j: (0, j)),
        name="adaln_mod",
    )(c, w_ada, b_ada.reshape(1, 6 * D))


def _mixer_kernel(x_ref, mod_ref, w_in_ref, b_in_ref, w_cvdw_ref, b_cvdw_ref, lncv_g_ref, lncv_b_ref,
                  w_cvout_ref, w_lconv_ref, b_lconv_ref, w_la_ref, b_la_ref, w_lx_ref, b_lx_ref,
                  lam_ref, w_lout_ref, w_o_ref, b_o_ref, ln1g_ref, ln1b_ref, w_r_ref, b_r_ref,
                  h1_ref, u2_ref, route_ref,
                  cvbuf, cvo_s, lbuf, a_s, u_s, hs_s, hstate):
    i = pl.program_id(0)

    @pl.when(i == 0)
    def _():
        cvbuf[0:CV_HIST, :] = jnp.zeros((CV_HIST, D), F32)
        lbuf[0:LRU_HIST, :] = jnp.zeros((LRU_HIST, D), F32)
        hstate[...] = jnp.zeros((BATCH, D), F32)

    def mod(j):
        return mod_ref[:, j * D:(j + 1) * D]

    x = x_ref[...]
    u = _per_batch(_layer_norm(x), 1.0 + mod(1), mod(0))
    ub = u.astype(BF16)

    def in_proj(lo, hi):
        return jnp.dot(ub, w_in_ref[:, lo:hi], preferred_element_type=F32) + b_in_ref[:, lo:hi]

    zc = in_proj(0, 2 * D)
    cvbuf[CV_HIST:CV_HIST + ROWS, :] = zc[:, :D] * jax.nn.sigmoid(zc[:, D:])

    def conv_chunk(rc, carry):
        r0 = pl.multiple_of(rc * CONV_CHUNK, CONV_CHUNK)
        for c in range(D // LANES):
            ls = slice(c * LANES, (c + 1) * LANES)
            acc = jnp.broadcast_to(b_cvdw_ref[:, ls], (CONV_CHUNK, LANES))
            for k in range(CONV_K):
                acc = acc + w_cvdw_ref[k:k + 1, ls] * cvbuf[pl.ds(r0 + BATCH * k, CONV_CHUNK), ls]
            cvo_s[pl.ds(r0, CONV_CHUNK), ls] = acc
        return carry

    lax.fori_loop(0, ROWS // CONV_CHUNK, conv_chunk, 0)
    cvbuf[0:CV_HIST, :] = cvbuf[ROWS:ROWS + CV_HIST, :]

    ya = _layer_norm(cvo_s[...]) * lncv_g_ref[...] + lncv_b_ref[...]
    ya = ya * jax.nn.sigmoid(ya)
    ya = _bdot(ya, w_cvout_ref[...])

    lbuf[LRU_HIST:LRU_HIST + ROWS, :] = in_proj(2 * D, 3 * D)
    xl = jnp.broadcast_to(b_lconv_ref[...], (ROWS, D))
    for k in range(LRU_CONV_K):
        xl = xl + w_lconv_ref[k:k + 1, :] * lbuf[BATCH * k:BATCH * k + ROWS, :]
    lbuf[0:LRU_HIST, :] = lbuf[ROWS:ROWS + LRU_HIST, :]

    ga, gx = [], []
    for h in range(LRU_HEADS):
        xh = xl[:, h * LRU_HEAD_DIM:(h + 1) * LRU_HEAD_DIM].astype(BF16)
        ga.append(jnp.dot(xh, w_la_ref[h], preferred_element_type=F32))
        gx.append(jnp.dot(xh, w_lx_ref[h], preferred_element_type=F32))
    gate_a = jax.nn.sigmoid(jnp.concatenate(ga, axis=-1) + b_la_ref[...])
    gate_x = jax.nn.sigmoid(jnp.concatenate(gx, axis=-1) + b_lx_ref[...])
    nlam = -lam_ref[...]
    softplus = jnp.maximum(nlam, 0.0) + jnp.log1p(jnp.exp(-jnp.abs(nlam)))
    log_a = (-LRU_C * gate_a) * softplus
    a = jnp.exp(log_a)
    mult = jnp.sqrt(1.0 - a * a)
    row = lax.broadcasted_iota(jnp.int32, (ROWS, D), 0) + i * ROWS
    mult = jnp.where(row < BATCH, 1.0, mult)
    a_s[...] = a
    u_s[...] = mult * (gate_x * xl)

    def scan_step(t, h):
        r = pl.multiple_of(t * BATCH, BATCH)
        h = a_s[pl.ds(r, BATCH), :] * h + u_s[pl.ds(r, BATCH), :]
        hs_s[pl.ds(r, BATCH), :] = h
        return h

    hstate[...] = lax.fori_loop(0, TS, scan_step, hstate[...], unroll=8)

    zg = in_proj(3 * D, 4 * D)
    gelu = 0.5 * zg * (1.0 + jnp.tanh(0.7978845608028654 * (zg + 0.044715 * (zg * zg * zg))))
    yb = _bdot(hs_s[...] * gelu, w_lout_ref[...])

    gm = jax.nn.sigmoid(in_proj(4 * D, 6 * D))
    mix = _bdot(gm[:, :D] * ya + gm[:, D:] * yb, w_o_ref[...]) + b_o_ref[...]
    h1 = _layer_norm(DEEPNORM_ALPHA * x + _per_batch(mix, mod(2))) * ln1g_ref[...] + ln1b_ref[...]
    h1_ref[...] = h1

    u2 = _per_batch(_layer_norm(h1), 1.0 + mod(4), mod(3))
    u2_ref[...] = u2
    logits = jnp.dot(u2, w_r_ref[...], preferred_element_type=F32, precision=HIGHEST) + b_r_ref[...]
    lane = lax.broadcasted_iota(jnp.int32, (ROWS, ROUTE_W), 1)
    vals, idxs = [], []
    for _ in range(TOP_K):
        m = jnp.max(logits, axis=-1, keepdims=True)
        idx = jnp.min(jnp.where(logits == m, lane, ROUTE_W), axis=-1, keepdims=True)
        vals.append(m)
        idxs.append(idx)
        logits = jnp.where(lane == idx, -jnp.inf, logits)
    exps = [jnp.exp(v - vals[0]) for v in vals]
    denom = exps[0] + exps[1] + exps[2] + exps[3]
    route = jnp.zeros((ROWS, ROUTE_W), F32)
    for k in range(TOP_K):
        route = jnp.where(lane == k, idxs[k].astype(F32), route)
        route = jnp.where(lane == TOP_K + k, exps[k] / denom, route)
    route_ref[...] = route


def _const_spec(shape):
    nd = len(shape)
    return pl.BlockSpec(shape, lambda i: (0,) * nd, pipeline_mode=pl.Buffered(1))


def _mixer(xt, mod, w_in, b_in, w_cvdw, b_cvdw, lncv_g, lncv_b, w_cvout, w_lconv, b_lconv,
           w_la, b_la, w_lx, b_lx, lam, w_lout, w_o, b_o, ln1g, ln1b, w_r, b_r):
    consts = (mod, w_in, b_in, w_cvdw, b_cvdw, lncv_g, lncv_b, w_cvout, w_lconv, b_lconv,
              w_la, b_la, w_lx, b_lx, lam, w_lout, w_o, b_o, ln1g, ln1b, w_r, b_r)
    row_spec = pl.BlockSpec((ROWS, D), lambda i: (i, 0))
    return pl.pallas_call(
        _mixer_kernel,
        out_shape=(jax.ShapeDtypeStruct((T, D), F32), jax.ShapeDtypeStruct((T, D), F32),
                   jax.ShapeDtypeStruct((T, ROUTE_W), F32)),
        grid=(SEQ // TS,),
        in_specs=[row_spec] + [_const_spec(a.shape) for a in consts],
        out_specs=(row_spec, row_spec, pl.BlockSpec((ROWS, ROUTE_W), lambda i: (i, 0))),
        scratch_shapes=[pltpu.VMEM((CV_HIST + ROWS, D), F32),
                        pltpu.VMEM((ROWS, D), F32),
                        pltpu.VMEM((LRU_HIST + ROWS, D), F32),
                        pltpu.VMEM((ROWS, D), F32),
                        pltpu.VMEM((ROWS, D), F32),
                        pltpu.VMEM((ROWS, D), F32),
                        pltpu.VMEM((BATCH, D), F32)],
        compiler_params=pltpu.CompilerParams(dimension_semantics=("arbitrary",),
                                             vmem_limit_bytes=58 * 1024 * 1024),
        name="mixer_router",
    )(xt, *consts)


def _dispatch_kernel(dest_ref, zrow_ref, zflag_ref, na_ref, u_ref, xb_hbm, zbuf, zsem, sem):
    i = pl.program_id(0)

    def zero_block(row):
        return pltpu.make_async_copy(zbuf, xb_hbm.at[pl.ds(pl.multiple_of(row, EBLK), EBLK)], zsem)

    @pl.when(i == 0)
    def _():
        zbuf[...] = jnp.zeros((EBLK, D), F32)
        for e in range(N_EXPERTS):
            @pl.when(zflag_ref[e] > 0)
            def _():
                zero_block(zrow_ref[e]).start()

        def tail_start(b, carry):
            zero_block(b * EBLK).start()
            return carry

        def tail_wait(b, carry):
            zero_block(b * EBLK).wait()
            return carry

        lax.fori_loop(na_ref[0], N_EBLK, tail_start, 0)
        for e in range(N_EXPERTS):
            @pl.when(zflag_ref[e] > 0)
            def _():
                zero_block(zrow_ref[e]).wait()
        lax.fori_loop(na_ref[0], N_EBLK, tail_wait, 0)

    def issue(r, carry):
        for k in range(TOP_K):
            d = dest_ref[(i * TOK_TILE + r) * TOP_K + k]
            pltpu.make_async_copy(u_ref.at[pl.ds(r, 1)], xb_hbm.at[pl.ds(d, 1)], sem).start()
        return carry

    lax.fori_loop(0, TOK_TILE, issue, 0)
    for k in range(TOP_K):
        pltpu.make_async_copy(u_ref, xb_hbm.at[pl.ds(0, TOK_TILE)], sem).wait()


def _dispatch(dest, zrow, zflag, n_active, u2):
    return pl.pallas_call(
        _dispatch_kernel,
        out_shape=jax.ShapeDtypeStruct((N_EROWS, D), F32),
        grid_spec=pltpu.PrefetchScalarGridSpec(
            num_scalar_prefetch=4,
            grid=(T // TOK_TILE,),
            in_specs=[pl.BlockSpec((TOK_TILE, D), lambda i, *_: (i, 0))],
            out_specs=pl.BlockSpec(memory_space=pl.ANY),
            scratch_shapes=[pltpu.VMEM((EBLK, D), F32), pltpu.SemaphoreType.DMA, pltpu.SemaphoreType.DMA],
        ),
        compiler_params=pltpu.CompilerParams(dimension_semantics=("arbitrary",)),
        name="moe_dispatch",
    )(dest, zrow, zflag, n_active, u2)


def _expert_kernel(be_ref, na_ref, x_ref, wg_ref, wl_ref, bg_ref, bl_ref, wd_ref, bd_ref, o_ref):
    @pl.when(pl.program_id(0) < na_ref[0])
    def _():
        xb = x_ref[...].astype(BF16)
        zg = jnp.dot(xb, wg_ref[0], preferred_element_type=F32) + bg_ref[0]
        zl = jnp.dot(xb, wl_ref[0], preferred_element_type=F32) + bl_ref[0]
        zg = jnp.minimum(zg, SWIGLU_LIMIT)
        zl = jnp.clip(zl, -SWIGLU_LIMIT, SWIGLU_LIMIT)
        act = zg * jax.nn.sigmoid(SWIGLU_ALPHA * zg) * (zl + 1.0)
        o_ref[...] = jnp.dot(act.astype(BF16), wd_ref[0], preferred_element_type=F32) + bd_ref[0]

    @pl.when(pl.program_id(0) >= na_ref[0])
    def _():
        o_ref[...] = jnp.zeros((EBLK, D), F32)


def _experts(block_e, n_active, xb, wg, wl, bg, bl, wd, bd):
    def row_map(b, be, na):
        return (jnp.minimum(b, na[0] - 1), 0)

    def w_map(b, be, na):
        return (be[b], 0, 0)

    return pl.pallas_call(
        _expert_kernel,
        out_shape=jax.ShapeDtypeStruct((N_EROWS, D), F32),
        grid_spec=pltpu.PrefetchScalarGridSpec(
            num_scalar_prefetch=2,
            grid=(N_EBLK,),
            in_specs=[pl.BlockSpec((EBLK, D), row_map),
                      pl.BlockSpec((1, D, D_EXPERT), w_map),
                      pl.BlockSpec((1, D, D_EXPERT), w_map),
                      pl.BlockSpec((1, 1, D_EXPERT), w_map),
                      pl.BlockSpec((1, 1, D_EXPERT), w_map),
                      pl.BlockSpec((1, D_EXPERT, D), w_map),
                      pl.BlockSpec((1, 1, D), w_map)],
            out_specs=pl.BlockSpec((EBLK, D), lambda b, be, na: (b, 0)),
        ),
        compiler_params=pltpu.CompilerParams(dimension_semantics=("arbitrary",),
                                             vmem_limit_bytes=40 * 1024 * 1024),
        name="moe_experts",
    )(block_e, n_active, xb, wg, wl, bg, bl, wd, bd)


def _combine_kernel(dest_ref, h1_ref, route_ref, g2_ref, lng_ref, lnb_ref, yb_hbm, o_ref, gbuf, sem):
    i = pl.program_id(0)

    def issue(r, carry):
        for k in range(TOP_K):
            d = dest_ref[(i * TOK_TILE + r) * TOP_K + k]
            pltpu.make_async_copy(yb_hbm.at[pl.ds(d, 1)], gbuf.at[k, pl.ds(r, 1)], sem).start()
        return carry

    lax.fori_loop(0, TOK_TILE, issue, 0)
    for k in range(TOP_K):
        pltpu.make_async_copy(yb_hbm.at[pl.ds(0, TOK_TILE)], gbuf.at[k], sem).wait()

    route = route_ref[...]
    y = route[:, TOP_K:TOP_K + 1] * gbuf[0]
    for k in range(1, TOP_K):
        y = y + route[:, TOP_K + k:TOP_K + k + 1] * gbuf[k]
    h = DEEPNORM_ALPHA * h1_ref[...] + _per_batch(y, g2_ref[...])
    o_ref[...] = _layer_norm(h) * lng_ref[...] + lnb_ref[...]


def _combine(dest, h1, route, g2, ln2g, ln2b, yb):
    row_spec = pl.BlockSpec((TOK_TILE, D), lambda i, *_: (i, 0))
    vec_spec = pl.BlockSpec((1, D), lambda i, *_: (0, 0))
    return pl.pallas_call(
        _combine_kernel,
        out_shape=jax.ShapeDtypeStruct((T, D), F32),
        grid_spec=pltpu.PrefetchScalarGridSpec(
            num_scalar_prefetch=1,
            grid=(T // TOK_TILE,),
            in_specs=[row_spec,
                      pl.BlockSpec((TOK_TILE, ROUTE_W), lambda i, *_: (i, 0)),
                      pl.BlockSpec((BATCH, D), lambda i, *_: (0, 0)),
                      vec_spec, vec_spec,
                      pl.BlockSpec(memory_space=pl.ANY)],
            out_specs=row_spec,
            scratch_shapes=[pltpu.VMEM((TOP_K, TOK_TILE, D), F32), pltpu.SemaphoreType.DMA],
        ),
        compiler_params=pltpu.CompilerParams(dimension_semantics=("arbitrary",)),
        name="moe_combine",
    )(dest, h1, route, g2, ln2g, ln2b, yb)


def _routing_plan(route):
    top_idx = route[:, :TOP_K].astype(jnp.int32)
    mask = (top_idx[:, :, None] == jnp.arange(N_EXPERTS, dtype=jnp.int32)).astype(jnp.int32).sum(1)
    counts = mask.sum(0)
    rank = jnp.cumsum(mask, axis=0) - mask
    padded = (counts + EBLK - 1) // EBLK * EBLK
    pend = jnp.cumsum(padded)
    pstart = pend - padded
    dest = pstart[top_idx] + jnp.take_along_axis(rank, top_idx, axis=1)
    n_active = (pend[-1] // EBLK).astype(jnp.int32).reshape(1)
    block_e = jnp.minimum(
        jnp.searchsorted(pend, jnp.arange(N_EBLK, dtype=jnp.int32) * EBLK, side='right'),
        N_EXPERTS - 1).astype(jnp.int32)
    zrow = jnp.maximum(pend - EBLK, 0).astype(jnp.int32)
    zflag = (padded > 0).astype(jnp.int32)
    return dest.reshape(-1).astype(jnp.int32), block_e, n_active, zrow, zflag


def kernel(x, c, w_ada, b_ada, w_in, b_in, w_cv_dw, b_cv_dw, ln_cv_g, ln_cv_b, w_cv_out, w_lru_conv, b_lru_conv, w_lru_a, b_lru_a, w_lru_x, b_lru_x, lru_lambda, w_lru_out, w_o, b_o, ln1_g, ln1_b, w_router, b_router, w_up, b_up, w_down, b_down, ln2_g, ln2_b):
    def vec(v):
        return v.reshape(1, -1)

    mod = _modulation(c, w_ada[0], b_ada[0])

    xt = jnp.transpose(x, (1, 0, 2)).reshape(T, D)
    w_r = jnp.pad(w_router[0], ((0, 0), (0, ROUTE_W - N_EXPERTS)))
    b_r = jnp.pad(b_router[0], (0, ROUTE_W - N_EXPERTS), constant_values=NEG_BIG)
    h1, u2, route = _mixer(
        xt, mod, w_in[0].astype(BF16), vec(b_in[0]), w_cv_dw[0], vec(b_cv_dw[0]), vec(ln_cv_g[0]),
        vec(ln_cv_b[0]), w_cv_out[0].astype(BF16), w_lru_conv[0], vec(b_lru_conv[0]),
        w_lru_a[0].astype(BF16), vec(b_lru_a[0]), w_lru_x[0].astype(BF16), vec(b_lru_x[0]),
        vec(lru_lambda[0]), w_lru_out[0].astype(BF16), w_o[0].astype(BF16), vec(b_o[0]),
        vec(ln1_g[0]), vec(ln1_b[0]), w_r, vec(b_r))

    dest, block_e, n_active, zrow, zflag = _routing_plan(route)
    xb = _dispatch(dest, zrow, zflag, n_active, u2)

    wg = w_up[0][:, :, 0::2].astype(BF16)
    wl = w_up[0][:, :, 1::2].astype(BF16)
    bg = b_up[0][:, 0::2].reshape(N_EXPERTS, 1, D_EXPERT)
    bl = b_up[0][:, 1::2].reshape(N_EXPERTS, 1, D_EXPERT)
    wd = w_down[0].astype(BF16)
    bd = b_down[0].reshape(N_EXPERTS, 1, D)
    yb = _experts(block_e, n_active, xb, wg, wl, bg, bl, wd, bd)

    out = _combine(dest, h1, route, mod[:, 5 * D:], vec(ln2_g[0]), vec(ln2_b[0]), yb)
    return jnp.transpose(out.reshape(SEQ, BATCH, D), (1, 0, 2))
```

```python
import functools

import jax
import jax.numpy as jnp
from jax import lax
from jax.experimental import pallas as pl
from jax.experimental.pallas import tpu as pltpu

F32 = jnp.float32
BF16 = jnp.bfloat16
HIGHEST = lax.Precision.HIGHEST

D = 1024
BATCH = 8
SEQ = 2048
T = BATCH * SEQ
CONV_K = 31
LRU_CONV_K = 4
LRU_HEADS = 8
LRU_HEAD_DIM = D // LRU_HEADS
LRU_C = 8.0
N_EXPERTS = 32
TOP_K = 4
D_EXPERT = D
SWIGLU_LIMIT = 7.0
SWIGLU_ALPHA = 1.702
LN_EPS = 1e-5
DEEPNORM_ALPHA = 2.0 ** 0.25
IN_COLS = 6 * D

LANES = 128
SUBLANES = 8
MXU_N = 256

TS = 64
ROWS = TS * BATCH
CV_HIST = (CONV_K - 1) * BATCH
LRU_HIST = (LRU_CONV_K - 1) * BATCH
CONV_CHUNK = 64

EBLK = 256
N_EBLK = (T * TOP_K + N_EXPERTS * (EBLK - 1) + EBLK - 1) // EBLK
N_EROWS = N_EBLK * EBLK
TOK_TILE = 256

ROUTE_W = LANES
NEG_BIG = -1e30


def _layer_norm(x):
    mu = jnp.mean(x, axis=-1, keepdims=True)
    xc = x - mu
    var = jnp.mean(xc * xc, axis=-1, keepdims=True)
    return xc * lax.rsqrt(var + LN_EPS)


def _per_batch(v, scale=None, shift=None):
    rows = v.shape[0]
    v3 = v.reshape(rows // BATCH, BATCH, D)
    if scale is not None:
        v3 = v3 * scale[None]
    if shift is not None:
        v3 = v3 + shift[None]
    return v3.reshape(rows, D)


def _bdot(a, w):
    return jnp.dot(a.astype(BF16), w, preferred_element_type=F32)


def _mod_kernel(c_ref, w_ref, b_ref, o_ref):
    c = c_ref[...]
    o_ref[...] = jnp.dot(c * jax.nn.sigmoid(c), w_ref[...], preferred_element_type=F32,
                         precision=HIGHEST) + b_ref[...]


def _modulation(c, w_ada, b_ada):
    tn = 1024
    return pl.pallas_call(
        _mod_kernel,
        out_shape=jax.ShapeDtypeStruct((BATCH, 6 * D), F32),
        grid=(6 * D // tn,),
        in_specs=[pl.BlockSpec((BATCH, D), lambda j: (0, 0)),
                  pl.BlockSpec((D, tn), lambda j: (0, j)),
                  pl.BlockSpec((1, tn), lambda j: (0, j))],
        out_specs=pl.BlockSpec((BATCH, tn), lambda j: (0, j)),
        name="adaln_mod",
    )(c, w_ada, b_ada.reshape(1, 6 * D))


def _mixer_kernel(x_ref, mod_ref, w_in_ref, b_in_ref, w_cvdw_ref, b_cvdw_ref, lncv_g_ref, lncv_b_ref,
                  w_cvout_ref, w_lconv_ref, b_lconv_ref, w_la_ref, b_la_ref, w_lx_ref, b_lx_ref,
                  lam_ref, w_lout_ref, w_o_ref, b_o_ref, ln1g_ref, ln1b_ref, w_r_ref, b_r_ref,
                  h1_ref, u2_ref, route_ref, counts_ref,
                  cvbuf, cvo_s, lbuf, a_s, u_s, hs_s, hstate, cnt_s):
    i = pl.program_id(0)

    @pl.when(i == 0)
    def _():
        cvbuf[0:CV_HIST, :] = jnp.zeros((CV_HIST, D), F32)
        lbuf[0:LRU_HIST, :] = jnp.zeros((LRU_HIST, D), F32)
        hstate[...] = jnp.zeros((BATCH, D), F32)
        cnt_s[...] = jnp.zeros((1, ROUTE_W), F32)

    def mod(j):
        return mod_ref[:, j * D:(j + 1) * D]

    x = x_ref[...]
    u = _per_batch(_layer_norm(x), 1.0 + mod(1), mod(0))
    ub = u.astype(BF16)

    def in_proj(lo, hi):
        return jnp.dot(ub, w_in_ref[:, lo:hi], preferred_element_type=F32) + b_in_ref[:, lo:hi]

    zc = in_proj(0, 2 * D)
    cvbuf[CV_HIST:CV_HIST + ROWS, :] = zc[:, :D] * jax.nn.sigmoid(zc[:, D:])

    def conv_chunk(rc, carry):
        r0 = pl.multiple_of(rc * CONV_CHUNK, CONV_CHUNK)
        for c in range(D // LANES):
            ls = slice(c * LANES, (c + 1) * LANES)
            acc = jnp.broadcast_to(b_cvdw_ref[:, ls], (CONV_CHUNK, LANES))
            for k in range(CONV_K):
                acc = acc + w_cvdw_ref[k:k + 1, ls] * cvbuf[pl.ds(r0 + BATCH * k, CONV_CHUNK), ls]
            cvo_s[pl.ds(r0, CONV_CHUNK), ls] = acc
        return carry

    lax.fori_loop(0, ROWS // CONV_CHUNK, conv_chunk, 0)
    cvbuf[0:CV_HIST, :] = cvbuf[ROWS:ROWS + CV_HIST, :]

    ya = _layer_norm(cvo_s[...]) * lncv_g_ref[...] + lncv_b_ref[...]
    ya = ya * jax.nn.sigmoid(ya)
    ya = _bdot(ya, w_cvout_ref[...])

    lbuf[LRU_HIST:LRU_HIST + ROWS, :] = in_proj(2 * D, 3 * D)
    xl = jnp.broadcast_to(b_lconv_ref[...], (ROWS, D))
    for k in range(LRU_CONV_K):
        xl = xl + w_lconv_ref[k:k + 1, :] * lbuf[BATCH * k:BATCH * k + ROWS, :]
    lbuf[0:LRU_HIST, :] = lbuf[ROWS:ROWS + LRU_HIST, :]

    ga, gx = [], []
    for h in range(LRU_HEADS):
        xh = xl[:, h * LRU_HEAD_DIM:(h + 1) * LRU_HEAD_DIM].astype(BF16)
        ga.append(jnp.dot(xh, w_la_ref[h], preferred_element_type=F32))
        gx.append(jnp.dot(xh, w_lx_ref[h], preferred_element_type=F32))
    gate_a = jax.nn.sigmoid(jnp.concatenate(ga, axis=-1) + b_la_ref[...])
    gate_x = jax.nn.sigmoid(jnp.concatenate(gx, axis=-1) + b_lx_ref[...])
    nlam = -lam_ref[...]
    softplus = jnp.maximum(nlam, 0.0) + jnp.log1p(jnp.exp(-jnp.abs(nlam)))
    log_a = (-LRU_C * gate_a) * softplus
    a = jnp.exp(log_a)
    mult = jnp.sqrt(1.0 - a * a)
    row = lax.broadcasted_iota(jnp.int32, (ROWS, D), 0) + i * ROWS
    mult = jnp.where(row < BATCH, 1.0, mult)
    a_s[...] = a
    u_s[...] = mult * (gate_x * xl)

    def scan_step(t, h):
        r = pl.multiple_of(t * BATCH, BATCH)
        h = a_s[pl.ds(r, BATCH), :] * h + u_s[pl.ds(r, BATCH), :]
        hs_s[pl.ds(r, BATCH), :] = h
        return h

    hstate[...] = lax.fori_loop(0, TS, scan_step, hstate[...], unroll=8)

    zg = in_proj(3 * D, 4 * D)
    gelu = 0.5 * zg * (1.0 + jnp.tanh(0.7978845608028654 * (zg + 0.044715 * (zg * zg * zg))))
    yb = _bdot(hs_s[...] * gelu, w_lout_ref[...])

    gm = jax.nn.sigmoid(in_proj(4 * D, 6 * D))
    mix = _bdot(gm[:, :D] * ya + gm[:, D:] * yb, w_o_ref[...]) + b_o_ref[...]
    h1 = _layer_norm(DEEPNORM_ALPHA * x + _per_batch(mix, mod(2))) * ln1g_ref[...] + ln1b_ref[...]
    h1_ref[...] = h1

    u2 = _per_batch(_layer_norm(h1), 1.0 + mod(4), mod(3))
    u2_ref[...] = u2
    logits = jnp.dot(u2, w_r_ref[...], preferred_element_type=F32, precision=HIGHEST) + b_r_ref[...]
    lane = lax.broadcasted_iota(jnp.int32, (ROWS, ROUTE_W), 1)
    vals, idxs = [], []
    for _ in range(TOP_K):
        m = jnp.max(logits, axis=-1, keepdims=True)
        idx = jnp.min(jnp.where(logits == m, lane, ROUTE_W), axis=-1, keepdims=True)
        vals.append(m)
        idxs.append(idx)
        logits = jnp.where(lane == idx, -jnp.inf, logits)
    exps = [jnp.exp(v - vals[0]) for v in vals]
    denom = exps[0] + exps[1] + exps[2] + exps[3]
    route = jnp.zeros((ROWS, ROUTE_W), F32)
    for k in range(TOP_K):
        route = jnp.where(lane == k, idxs[k].astype(F32), route)
        route = jnp.where(lane == TOP_K + k, exps[k] / denom, route)

    sel = jnp.zeros((ROWS, ROUTE_W), F32)
    for k in range(TOP_K):
        sel = jnp.where(lane == idxs[k], 1.0, sel)
    tri = (lax.broadcasted_iota(jnp.int32, (ROWS, ROWS), 0)
           >= lax.broadcasted_iota(jnp.int32, (ROWS, ROWS), 1)).astype(BF16)
    seen = jnp.dot(tri, sel.astype(BF16), preferred_element_type=F32) + cnt_s[...]
    for k in range(TOP_K):
        slot = jnp.sum(jnp.where(lane == idxs[k], seen - 1.0, 0.0), axis=-1, keepdims=True)
        route = jnp.where(lane == 2 * TOP_K + k, slot, route)
    route_ref[...] = route
    cnt_s[...] = seen[ROWS - 1:ROWS, :]
    counts_ref[...] = jnp.broadcast_to(seen[ROWS - 1:ROWS, :], (SUBLANES, ROUTE_W))


def _const_spec(shape):
    nd = len(shape)
    return pl.BlockSpec(shape, lambda i: (0,) * nd, pipeline_mode=pl.Buffered(1))


def _mixer(xt, mod, w_in, b_in, w_cvdw, b_cvdw, lncv_g, lncv_b, w_cvout, w_lconv, b_lconv,
           w_la, b_la, w_lx, b_lx, lam, w_lout, w_o, b_o, ln1g, ln1b, w_r, b_r):
    consts = (mod, w_in, b_in, w_cvdw, b_cvdw, lncv_g, lncv_b, w_cvout, w_lconv, b_lconv,
              w_la, b_la, w_lx, b_lx, lam, w_lout, w_o, b_o, ln1g, ln1b, w_r, b_r)
    row_spec = pl.BlockSpec((ROWS, D), lambda i: (i, 0))
    return pl.pallas_call(
        _mixer_kernel,
        out_shape=(jax.ShapeDtypeStruct((T, D), F32), jax.ShapeDtypeStruct((T, D), F32),
                   jax.ShapeDtypeStruct((T, ROUTE_W), F32),
                   jax.ShapeDtypeStruct((SUBLANES, ROUTE_W), F32)),
        grid=(SEQ // TS,),
        in_specs=[row_spec] + [_const_spec(a.shape) for a in consts],
        out_specs=(row_spec, row_spec, pl.BlockSpec((ROWS, ROUTE_W), lambda i: (i, 0)),
                   pl.BlockSpec((SUBLANES, ROUTE_W), lambda i: (0, 0))),
        scratch_shapes=[pltpu.VMEM((CV_HIST + ROWS, D), F32),
                        pltpu.VMEM((ROWS, D), F32),
                        pltpu.VMEM((LRU_HIST + ROWS, D), F32),
                        pltpu.VMEM((ROWS, D), F32),
                        pltpu.VMEM((ROWS, D), F32),
                        pltpu.VMEM((ROWS, D), F32),
                        pltpu.VMEM((BATCH, D), F32),
                        pltpu.VMEM((1, ROUTE_W), F32)],
        compiler_params=pltpu.CompilerParams(dimension_semantics=("arbitrary",),
                                             vmem_limit_bytes=58 * 1024 * 1024),
        name="mixer_router",
    )(xt, *consts)


def _dispatch_kernel(dest_ref, zrow_ref, zflag_ref, na_ref, u_ref, xb_hbm, zbuf, zsem, sem):
    i = pl.program_id(0)

    def zero_block(row):
        return pltpu.make_async_copy(zbuf, xb_hbm.at[pl.ds(pl.multiple_of(row, EBLK), EBLK)], zsem)

    @pl.when(i == 0)
    def _():
        zbuf[...] = jnp.zeros((EBLK, D), F32)
        for e in range(N_EXPERTS):
            @pl.when(zflag_ref[e] > 0)
            def _():
                zero_block(zrow_ref[e]).start()

        def tail_start(b, carry):
            zero_block(b * EBLK).start()
            return carry

        def tail_wait(b, carry):
            zero_block(b * EBLK).wait()
            return carry

        lax.fori_loop(na_ref[0], N_EBLK, tail_start, 0)
        for e in range(N_EXPERTS):
            @pl.when(zflag_ref[e] > 0)
            def _():
                zero_block(zrow_ref[e]).wait()
        lax.fori_loop(na_ref[0], N_EBLK, tail_wait, 0)

    def issue(r, carry):
        for k in range(TOP_K):
            d = dest_ref[(i * TOK_TILE + r) * TOP_K + k]
            pltpu.make_async_copy(u_ref.at[pl.ds(r, 1)], xb_hbm.at[pl.ds(d, 1)], sem).start()
        return carry

    lax.fori_loop(0, TOK_TILE, issue, 0)
    for k in range(TOP_K):
        pltpu.make_async_copy(u_ref, xb_hbm.at[pl.ds(0, TOK_TILE)], sem).wait()


def _dispatch(dest, zrow, zflag, n_active, u2):
    return pl.pallas_call(
        _dispatch_kernel,
        out_shape=jax.ShapeDtypeStruct((N_EROWS, D), F32),
        grid_spec=pltpu.PrefetchScalarGridSpec(
            num_scalar_prefetch=4,
            grid=(T // TOK_TILE,),
            in_specs=[pl.BlockSpec((TOK_TILE, D), lambda i, *_: (i, 0))],
            out_specs=pl.BlockSpec(memory_space=pl.ANY),
            scratch_shapes=[pltpu.VMEM((EBLK, D), F32), pltpu.SemaphoreType.DMA, pltpu.SemaphoreType.DMA],
        ),
        compiler_params=pltpu.CompilerParams(dimension_semantics=("arbitrary",)),
        name="moe_dispatch",
    )(dest, zrow, zflag, n_active, u2)


def _expert_kernel(be_ref, na_ref, x_ref, wup_ref, bg_ref, bl_ref, wdn_ref, bd_ref, perm_ref, o_ref,
                   wg_s, wl_s, wd_s):
    b = pl.program_id(0)
    active = b < na_ref[0]
    new_expert = jnp.logical_or(b == 0, be_ref[b] != be_ref[jnp.maximum(b - 1, 0)])

    @pl.when(jnp.logical_and(active, new_expert))
    def _():
        for s in range(2 * D_EXPERT // MXU_N):
            slab = wup_ref[0, :, s * MXU_N:(s + 1) * MXU_N].astype(BF16)
            split = jnp.dot(slab, perm_ref[...], preferred_element_type=F32).astype(BF16)
            half = MXU_N // 2
            wg_s[:, s * half:(s + 1) * half] = split[:, :half]
            wl_s[:, s * half:(s + 1) * half] = split[:, half:]
        wd_s[...] = wdn_ref[0].astype(BF16)

    @pl.when(active)
    def _():
        xb = x_ref[...].astype(BF16)
        zg = jnp.dot(xb, wg_s[...], preferred_element_type=F32) + bg_ref[0]
        zl = jnp.dot(xb, wl_s[...], preferred_element_type=F32) + bl_ref[0]
        zg = jnp.minimum(zg, SWIGLU_LIMIT)
        zl = jnp.clip(zl, -SWIGLU_LIMIT, SWIGLU_LIMIT)
        act = zg * jax.nn.sigmoid(SWIGLU_ALPHA * zg) * (zl + 1.0)
        o_ref[...] = jnp.dot(act.astype(BF16), wd_s[...], preferred_element_type=F32) + bd_ref[0]

    @pl.when(jnp.logical_not(active))
    def _():
        o_ref[...] = jnp.zeros((EBLK, D), F32)


def _experts(block_e, n_active, xb, w_up, bg, bl, w_down, bd):
    def row_map(b, be, na):
        return (jnp.minimum(b, na[0] - 1), 0)

    def w_map(b, be, na):
        return (be[b], 0, 0)

    src = jnp.arange(MXU_N, dtype=jnp.int32)[:, None]
    dst = jnp.arange(MXU_N, dtype=jnp.int32)[None, :]
    half = MXU_N // 2
    perm = jnp.where(dst < half, src == 2 * dst, src == 2 * (dst - half) + 1).astype(BF16)

    return pl.pallas_call(
        _expert_kernel,
        out_shape=jax.ShapeDtypeStruct((N_EROWS, D), F32),
        grid_spec=pltpu.PrefetchScalarGridSpec(
            num_scalar_prefetch=2,
            grid=(N_EBLK,),
            in_specs=[pl.BlockSpec((EBLK, D), row_map),
                      pl.BlockSpec((1, D, 2 * D_EXPERT), w_map),
                      pl.BlockSpec((1, 1, D_EXPERT), w_map),
                      pl.BlockSpec((1, 1, D_EXPERT), w_map),
                      pl.BlockSpec((1, D_EXPERT, D), w_map),
                      pl.BlockSpec((1, 1, D), w_map),
                      pl.BlockSpec((MXU_N, MXU_N), lambda b, be, na: (0, 0))],
            out_specs=pl.BlockSpec((EBLK, D), lambda b, be, na: (b, 0)),
            scratch_shapes=[pltpu.VMEM((D, D_EXPERT), BF16), pltpu.VMEM((D, D_EXPERT), BF16),
                            pltpu.VMEM((D_EXPERT, D), BF16)],
        ),
        compiler_params=pltpu.CompilerParams(dimension_semantics=("arbitrary",),
                                             vmem_limit_bytes=52 * 1024 * 1024),
        name="moe_experts",
    )(block_e, n_active, xb, w_up, bg, bl, w_down, bd, perm)


def _combine_kernel(dest_ref, h1_ref, route_ref, g2_ref, lng_ref, lnb_ref, yb_hbm, o_ref, gbuf, sem):
    i = pl.program_id(0)

    def issue(r, carry):
        for k in range(TOP_K):
            d = dest_ref[(i * TOK_TILE + r) * TOP_K + k]
            pltpu.make_async_copy(yb_hbm.at[pl.ds(d, 1)], gbuf.at[k, pl.ds(r, 1)], sem).start()
        return carry

    lax.fori_loop(0, TOK_TILE, issue, 0)
    for k in range(TOP_K):
        pltpu.make_async_copy(yb_hbm.at[pl.ds(0, TOK_TILE)], gbuf.at[k], sem).wait()

    route = route_ref[...]
    y = route[:, TOP_K:TOP_K + 1] * gbuf[0]
    for k in range(1, TOP_K):
        y = y + route[:, TOP_K + k:TOP_K + k + 1] * gbuf[k]
    h = DEEPNORM_ALPHA * h1_ref[...] + _per_batch(y, g2_ref[...])
    o_ref[...] = _layer_norm(h) * lng_ref[...] + lnb_ref[...]


def _combine(dest, h1, route, g2, ln2g, ln2b, yb):
    row_spec = pl.BlockSpec((TOK_TILE, D), lambda i, *_: (i, 0))
    vec_spec = pl.BlockSpec((1, D), lambda i, *_: (0, 0))
    return pl.pallas_call(
        _combine_kernel,
        out_shape=jax.ShapeDtypeStruct((T, D), F32),
        grid_spec=pltpu.PrefetchScalarGridSpec(
            num_scalar_prefetch=1,
            grid=(T // TOK_TILE,),
            in_specs=[row_spec,
                      pl.BlockSpec((TOK_TILE, ROUTE_W), lambda i, *_: (i, 0)),
                      pl.BlockSpec((BATCH, D), lambda i, *_: (0, 0)),
                      vec_spec, vec_spec,
                      pl.BlockSpec(memory_space=pl.ANY)],
            out_specs=row_spec,
            scratch_shapes=[pltpu.VMEM((TOP_K, TOK_TILE, D), F32), pltpu.SemaphoreType.DMA],
        ),
        compiler_params=pltpu.CompilerParams(dimension_semantics=("arbitrary",)),
        name="moe_combine",
    )(dest, h1, route, g2, ln2g, ln2b, yb)


def _routing_plan(route, counts):
    top_idx = route[:, :TOP_K].astype(jnp.int32)
    slot = route[:, 2 * TOP_K:3 * TOP_K].astype(jnp.int32)
    counts = counts[0, :N_EXPERTS].astype(jnp.int32)
    padded = (counts + EBLK - 1) // EBLK * EBLK
    pend = jnp.cumsum(padded)
    pstart = pend - padded
    dest = pstart[top_idx] + slot
    n_active = (pend[-1] // EBLK).astype(jnp.int32).reshape(1)
    blk_start = jnp.arange(N_EBLK, dtype=jnp.int32) * EBLK
    block_e = jnp.minimum((pend[None, :] <= blk_start[:, None]).astype(jnp.int32).sum(1), N_EXPERTS - 1)
    zrow = jnp.maximum(pend - EBLK, 0).astype(jnp.int32)
    zflag = (padded > 0).astype(jnp.int32)
    return dest.reshape(-1).astype(jnp.int32), block_e.astype(jnp.int32), n_active, zrow, zflag


def kernel(x, c, w_ada, b_ada, w_in, b_in, w_cv_dw, b_cv_dw, ln_cv_g, ln_cv_b, w_cv_out, w_lru_conv, b_lru_conv, w_lru_a, b_lru_a, w_lru_x, b_lru_x, lru_lambda, w_lru_out, w_o, b_o, ln1_g, ln1_b, w_router, b_router, w_up, b_up, w_down, b_down, ln2_g, ln2_b):
    def vec(v):
        return v.reshape(1, -1)

    mod = _modulation(c, w_ada[0], b_ada[0])

    xt = jnp.transpose(x, (1, 0, 2)).reshape(T, D)
    w_r = jnp.pad(w_router[0], ((0, 0), (0, ROUTE_W - N_EXPERTS)))
    b_r = jnp.pad(b_router[0], (0, ROUTE_W - N_EXPERTS), constant_values=NEG_BIG)
    h1, u2, route, counts = _mixer(
        xt, mod, w_in[0].astype(BF16), vec(b_in[0]), w_cv_dw[0], vec(b_cv_dw[0]), vec(ln_cv_g[0]),
        vec(ln_cv_b[0]), w_cv_out[0].astype(BF16), w_lru_conv[0], vec(b_lru_conv[0]),
        w_lru_a[0].astype(BF16), vec(b_lru_a[0]), w_lru_x[0].astype(BF16), vec(b_lru_x[0]),
        vec(lru_lambda[0]), w_lru_out[0].astype(BF16), w_o[0].astype(BF16), vec(b_o[0]),
        vec(ln1_g[0]), vec(ln1_b[0]), w_r, vec(b_r))

    dest, block_e, n_active, zrow, zflag = _routing_plan(route, counts)
    xb = _dispatch(dest, zrow, zflag, n_active, u2)

    bg = b_up[0][:, 0::2].reshape(N_EXPERTS, 1, D_EXPERT)
    bl = b_up[0][:, 1::2].reshape(N_EXPERTS, 1, D_EXPERT)
    bd = b_down[0].reshape(N_EXPERTS, 1, D)
    yb = _experts(block_e, n_active, xb, w_up[0], bg, bl, w_down[0], bd)

    out = _combine(dest, h1, route, mod[:, 5 * D:], vec(ln2_g[0]), vec(ln2_b[0]), yb)
    return jnp.transpose(out.reshape(SEQ, BATCH, D), (1, 0, 2))
```

```python
import functools

import jax
import jax.numpy as jnp
from jax import lax
from jax.experimental import pallas as pl
from jax.experimental.pallas import tpu as pltpu

F32 = jnp.float32
BF16 = jnp.bfloat16
HIGHEST = lax.Precision.HIGHEST

D = 1024
BATCH = 8
SEQ = 2048
T = BATCH * SEQ
CONV_K = 31
LRU_CONV_K = 4
LRU_HEADS = 8
LRU_HEAD_DIM = D // LRU_HEADS
LRU_C = 8.0
N_EXPERTS = 32
TOP_K = 4
D_EXPERT = D
SWIGLU_LIMIT = 7.0
SWIGLU_ALPHA = 1.702
LN_EPS = 1e-5
DEEPNORM_ALPHA = 2.0 ** 0.25
IN_COLS = 6 * D

LANES = 128
SUBLANES = 8
MXU_N = 256

TS = 64
ROWS = TS * BATCH
CV_HIST = (CONV_K - 1) * BATCH
LRU_HIST = (LRU_CONV_K - 1) * BATCH
CONV_CHUNK = 64

EBLK = 512
N_EBLK = (T * TOP_K + N_EXPERTS * (EBLK - 1) + EBLK - 1) // EBLK
N_EROWS = N_EBLK * EBLK
TOK_TILE = 256

ROUTE_W = LANES
NEG_BIG = -1e30


def _layer_norm(x):
    mu = jnp.mean(x, axis=-1, keepdims=True)
    xc = x - mu
    var = jnp.mean(xc * xc, axis=-1, keepdims=True)
    return xc * lax.rsqrt(var + LN_EPS)


def _per_batch(v, scale=None, shift=None):
    rows = v.shape[0]
    v3 = v.reshape(rows // BATCH, BATCH, D)
    if scale is not None:
        v3 = v3 * scale[None]
    if shift is not None:
        v3 = v3 + shift[None]
    return v3.reshape(rows, D)


def _bdot(a, w):
    return jnp.dot(a.astype(BF16), w, preferred_element_type=F32)


def _mod_kernel(c_ref, w_ref, b_ref, o_ref):
    c = c_ref[...]
    o_ref[...] = jnp.dot(c * jax.nn.sigmoid(c), w_ref[...], preferred_element_type=F32,
                         precision=HIGHEST) + b_ref[...]


def _modulation(c, w_ada, b_ada):
    tn = 1024
    return pl.pallas_call(
        _mod_kernel,
        out_shape=jax.ShapeDtypeStruct((BATCH, 6 * D), F32),
        grid=(6 * D // tn,),
        in_specs=[pl.BlockSpec((BATCH, D), lambda j: (0, 0)),
                  pl.BlockSpec((D, tn), lambda j: (0, j)),
                  pl.BlockSpec((1, tn), lambda j: (0, j))],
        out_specs=pl.BlockSpec((BATCH, tn), lambda j: (0, j)),
        name="adaln_mod",
    )(c, w_ada, b_ada.reshape(1, 6 * D))


def _mixer_kernel(x_ref, mod_ref, w_in_ref, b_in_ref, w_cvdw_ref, b_cvdw_ref, lncv_g_ref, lncv_b_ref,
                  w_cvout_ref, w_lconv_ref, b_lconv_ref, w_la_ref, b_la_ref, w_lx_ref, b_lx_ref,
                  lam_ref, w_lout_ref, w_o_ref, b_o_ref, ln1g_ref, ln1b_ref, w_r_ref, b_r_ref,
                  h1_ref, u2_ref, route_ref, counts_ref,
                  cvbuf, cvo_s, lbuf, a_s, u_s, hs_s, hstate, cnt_s):
    i = pl.program_id(0)

    @pl.when(i == 0)
    def _():
        cvbuf[0:CV_HIST, :] = jnp.zeros((CV_HIST, D), F32)
        lbuf[0:LRU_HIST, :] = jnp.zeros((LRU_HIST, D), F32)
        hstate[...] = jnp.zeros((BATCH, D), F32)
        cnt_s[...] = jnp.zeros((1, ROUTE_W), F32)

    def mod(j):
        return mod_ref[:, j * D:(j + 1) * D]

    x = pltpu.einshape("bsd->sbd", x_ref[...]).reshape(ROWS, D)
    u = _per_batch(_layer_norm(x), 1.0 + mod(1), mod(0))
    ub = u.astype(BF16)

    def in_proj(lo, hi):
        return jnp.dot(ub, w_in_ref[:, lo:hi], preferred_element_type=F32) + b_in_ref[:, lo:hi]

    zc = in_proj(0, 2 * D)
    cvbuf[CV_HIST:CV_HIST + ROWS, :] = zc[:, :D] * jax.nn.sigmoid(zc[:, D:])
    lbuf[LRU_HIST:LRU_HIST + ROWS, :] = in_proj(2 * D, 3 * D)
    zg = in_proj(3 * D, 4 * D)
    zm = in_proj(4 * D, 6 * D)

    for rc in range(ROWS // CONV_CHUNK):
        r0 = rc * CONV_CHUNK
        for c in range(D // LANES):
            ls = slice(c * LANES, (c + 1) * LANES)
            acc = jnp.broadcast_to(b_cvdw_ref[:, ls], (CONV_CHUNK, LANES))
            for k in range(CONV_K):
                acc = acc + w_cvdw_ref[k:k + 1, ls] * cvbuf[r0 + BATCH * k:r0 + BATCH * k + CONV_CHUNK, ls]
            cvo_s[r0:r0 + CONV_CHUNK, ls] = acc
    cvbuf[0:CV_HIST, :] = cvbuf[ROWS:ROWS + CV_HIST, :]

    ya = _layer_norm(cvo_s[...]) * lncv_g_ref[...] + lncv_b_ref[...]
    ya = ya * jax.nn.sigmoid(ya)
    ya = _bdot(ya, w_cvout_ref[...])

    xl = jnp.broadcast_to(b_lconv_ref[...], (ROWS, D))
    for k in range(LRU_CONV_K):
        xl = xl + w_lconv_ref[k:k + 1, :] * lbuf[BATCH * k:BATCH * k + ROWS, :]
    lbuf[0:LRU_HIST, :] = lbuf[ROWS:ROWS + LRU_HIST, :]

    ga, gx = [], []
    for h in range(LRU_HEADS):
        xh = xl[:, h * LRU_HEAD_DIM:(h + 1) * LRU_HEAD_DIM].astype(BF16)
        ga.append(jnp.dot(xh, w_la_ref[h], preferred_element_type=F32))
        gx.append(jnp.dot(xh, w_lx_ref[h], preferred_element_type=F32))
    gate_a = jax.nn.sigmoid(jnp.concatenate(ga, axis=-1) + b_la_ref[...])
    gate_x = jax.nn.sigmoid(jnp.concatenate(gx, axis=-1) + b_lx_ref[...])
    nlam = -lam_ref[...]
    softplus = jnp.maximum(nlam, 0.0) + jnp.log1p(jnp.exp(-jnp.abs(nlam)))
    log_a = (-LRU_C * gate_a) * softplus
    a = jnp.exp(log_a)
    mult = jnp.sqrt(1.0 - a * a)
    row = lax.broadcasted_iota(jnp.int32, (ROWS, D), 0) + i * ROWS
    mult = jnp.where(row < BATCH, 1.0, mult)
    a_s[...] = a
    u_s[...] = mult * (gate_x * xl)

    h = hstate[...]
    for t in range(TS):
        h = a_s[t * BATCH:(t + 1) * BATCH, :] * h + u_s[t * BATCH:(t + 1) * BATCH, :]
        hs_s[t * BATCH:(t + 1) * BATCH, :] = h
    hstate[...] = h

    gelu = 0.5 * zg * (1.0 + jnp.tanh(0.7978845608028654 * (zg + 0.044715 * (zg * zg * zg))))
    yb = _bdot(hs_s[...] * gelu, w_lout_ref[...])

    gm = jax.nn.sigmoid(zm)
    mix = _bdot(gm[:, :D] * ya + gm[:, D:] * yb, w_o_ref[...]) + b_o_ref[...]
    h1 = _layer_norm(DEEPNORM_ALPHA * x + _per_batch(mix, mod(2))) * ln1g_ref[...] + ln1b_ref[...]
    h1_ref[...] = h1

    u2 = _per_batch(_layer_norm(h1), 1.0 + mod(4), mod(3))
    u2_ref[...] = u2
    u2_hi = u2.astype(BF16)
    u2_lo = (u2 - u2_hi.astype(F32)).astype(BF16)
    p_hi = jnp.dot(u2_hi, w_r_ref[...], preferred_element_type=F32)
    p_lo = jnp.dot(u2_lo, w_r_ref[...], preferred_element_type=F32)
    logits = (p_hi[:, :ROUTE_W] + (p_hi[:, ROUTE_W:] + p_lo[:, :ROUTE_W] + p_lo[:, ROUTE_W:])) + b_r_ref[...]
    lane = lax.broadcasted_iota(jnp.int32, (ROWS, ROUTE_W), 1)
    vals, idxs = [], []
    for _ in range(TOP_K):
        m = jnp.max(logits, axis=-1, keepdims=True)
        idx = jnp.min(jnp.where(logits == m, lane, ROUTE_W), axis=-1, keepdims=True)
        vals.append(m)
        idxs.append(idx)
        logits = jnp.where(lane == idx, -jnp.inf, logits)
    exps = [jnp.exp(v - vals[0]) for v in vals]
    denom = exps[0] + exps[1] + exps[2] + exps[3]
    route = jnp.zeros((ROWS, ROUTE_W), F32)
    for k in range(TOP_K):
        route = jnp.where(lane == k, idxs[k].astype(F32), route)
        route = jnp.where(lane == TOP_K + k, exps[k] / denom, route)

    sel = jnp.zeros((ROWS, ROUTE_W), F32)
    for k in range(TOP_K):
        sel = jnp.where(lane == idxs[k], 1.0, sel)
    tri = (lax.broadcasted_iota(jnp.int32, (ROWS, ROWS), 0)
           >= lax.broadcasted_iota(jnp.int32, (ROWS, ROWS), 1)).astype(BF16)
    seen = jnp.dot(tri, sel.astype(BF16), preferred_element_type=F32) + cnt_s[...]
    for k in range(TOP_K):
        slot = jnp.sum(jnp.where(lane == idxs[k], seen - 1.0, 0.0), axis=-1, keepdims=True)
        route = jnp.where(lane == 2 * TOP_K + k, slot, route)
    route_ref[...] = route
    cnt_s[...] = seen[ROWS - 1:ROWS, :]
    counts_ref[...] = jnp.broadcast_to(seen[ROWS - 1:ROWS, :], (SUBLANES, ROUTE_W))


def _const_spec(shape):
    nd = len(shape)
    return pl.BlockSpec(shape, lambda i: (0,) * nd, pipeline_mode=pl.Buffered(1))


def _mixer(xt, mod, w_in, b_in, w_cvdw, b_cvdw, lncv_g, lncv_b, w_cvout, w_lconv, b_lconv,
           w_la, b_la, w_lx, b_lx, lam, w_lout, w_o, b_o, ln1g, ln1b, w_r, b_r):
    consts = (mod, w_in, b_in, w_cvdw, b_cvdw, lncv_g, lncv_b, w_cvout, w_lconv, b_lconv,
              w_la, b_la, w_lx, b_lx, lam, w_lout, w_o, b_o, ln1g, ln1b, w_r, b_r)
    row_spec = pl.BlockSpec((ROWS, D), lambda i: (i, 0))
    return pl.pallas_call(
        _mixer_kernel,
        out_shape=(jax.ShapeDtypeStruct((T, D), F32), jax.ShapeDtypeStruct((T, D), F32),
                   jax.ShapeDtypeStruct((T, ROUTE_W), F32),
                   jax.ShapeDtypeStruct((SUBLANES, ROUTE_W), F32)),
        grid=(SEQ // TS,),
        in_specs=[pl.BlockSpec((BATCH, TS, D), lambda i: (0, i, 0))] + [_const_spec(a.shape) for a in consts],
        out_specs=(row_spec, row_spec, pl.BlockSpec((ROWS, ROUTE_W), lambda i: (i, 0)),
                   pl.BlockSpec((SUBLANES, ROUTE_W), lambda i: (0, 0))),
        scratch_shapes=[pltpu.VMEM((CV_HIST + ROWS, D), F32),
                        pltpu.VMEM((ROWS, D), F32),
                        pltpu.VMEM((LRU_HIST + ROWS, D), F32),
                        pltpu.VMEM((ROWS, D), F32),
                        pltpu.VMEM((ROWS, D), F32),
                        pltpu.VMEM((ROWS, D), F32),
                        pltpu.VMEM((BATCH, D), F32),
                        pltpu.VMEM((1, ROUTE_W), F32)],
        compiler_params=pltpu.CompilerParams(dimension_semantics=("arbitrary",),
                                             vmem_limit_bytes=58 * 1024 * 1024),
        name="mixer_router",
    )(xt, *consts)


def _dispatch_kernel(dest_ref, zrow_ref, zflag_ref, na_ref, u_ref, xb_hbm, zbuf, zsem, sem):
    i = pl.program_id(0)

    def zero_block(row):
        return pltpu.make_async_copy(zbuf, xb_hbm.at[pl.ds(pl.multiple_of(row, EBLK), EBLK)], zsem)

    @pl.when(i == 0)
    def _():
        zbuf[...] = jnp.zeros((EBLK, D), F32)
        for e in range(N_EXPERTS):
            @pl.when(zflag_ref[e] > 0)
            def _():
                zero_block(zrow_ref[e]).start()

        def tail_start(b, carry):
            zero_block(b * EBLK).start()
            return carry

        def tail_wait(b, carry):
            zero_block(b * EBLK).wait()
            return carry

        lax.fori_loop(na_ref[0], N_EBLK, tail_start, 0)
        for e in range(N_EXPERTS):
            @pl.when(zflag_ref[e] > 0)
            def _():
                zero_block(zrow_ref[e]).wait()
        lax.fori_loop(na_ref[0], N_EBLK, tail_wait, 0)

    def issue(r, carry):
        for k in range(TOP_K):
            d = dest_ref[(i * TOK_TILE + r) * TOP_K + k]
            pltpu.make_async_copy(u_ref.at[pl.ds(r, 1)], xb_hbm.at[pl.ds(d, 1)], sem).start(priority=k % 2)
        return carry

    lax.fori_loop(0, TOK_TILE, issue, 0, unroll=8)
    for k in range(TOP_K):
        pltpu.make_async_copy(u_ref, xb_hbm.at[pl.ds(0, TOK_TILE)], sem).wait()


def _dispatch(dest, zrow, zflag, n_active, u2):
    return pl.pallas_call(
        _dispatch_kernel,
        out_shape=jax.ShapeDtypeStruct((N_EROWS, D), F32),
        grid_spec=pltpu.PrefetchScalarGridSpec(
            num_scalar_prefetch=4,
            grid=(T // TOK_TILE,),
            in_specs=[pl.BlockSpec((TOK_TILE, D), lambda i, *_: (i, 0))],
            out_specs=pl.BlockSpec(memory_space=pl.ANY),
            scratch_shapes=[pltpu.VMEM((EBLK, D), F32), pltpu.SemaphoreType.DMA, pltpu.SemaphoreType.DMA],
        ),
        compiler_params=pltpu.CompilerParams(dimension_semantics=("arbitrary",)),
        name="moe_dispatch",
    )(dest, zrow, zflag, n_active, u2)


def _expert_kernel(be_ref, na_ref, x_ref, wup_ref, bg_ref, bl_ref, wdn_ref, bd_ref, perm_ref, o_ref,
                   wg_s, wl_s, wd_s):
    b = pl.program_id(0)
    active = b < na_ref[0]
    new_expert = jnp.logical_or(b == 0, be_ref[b] != be_ref[jnp.maximum(b - 1, 0)])

    @pl.when(jnp.logical_and(active, new_expert))
    def _():
        for s in range(2 * D_EXPERT // MXU_N):
            slab = wup_ref[0, :, s * MXU_N:(s + 1) * MXU_N].astype(BF16)
            split = jnp.dot(slab, perm_ref[...], preferred_element_type=F32).astype(BF16)
            half = MXU_N // 2
            wg_s[:, s * half:(s + 1) * half] = split[:, :half]
            wl_s[:, s * half:(s + 1) * half] = split[:, half:]
        wd_s[...] = wdn_ref[0].astype(BF16)

    @pl.when(active)
    def _():
        xb = x_ref[...].astype(BF16)
        zg = jnp.dot(xb, wg_s[...], preferred_element_type=F32) + bg_ref[0]
        zl = jnp.dot(xb, wl_s[...], preferred_element_type=F32) + bl_ref[0]
        zg = jnp.minimum(zg, SWIGLU_LIMIT)
        zl = jnp.clip(zl, -SWIGLU_LIMIT, SWIGLU_LIMIT)
        act = zg * jax.nn.sigmoid(SWIGLU_ALPHA * zg) * (zl + 1.0)
        o_ref[...] = jnp.dot(act.astype(BF16), wd_s[...], preferred_element_type=F32) + bd_ref[0]

    @pl.when(jnp.logical_not(active))
    def _():
        o_ref[...] = jnp.zeros((EBLK, D), F32)


def _experts(block_e, n_active, xb, w_up, bg, bl, w_down, bd):
    def row_map(b, be, na):
        return (jnp.maximum(jnp.minimum(b, na[0] - 1), 0), 0)

    def w_map(b, be, na):
        return (be[b], 0, 0)

    src = jnp.arange(MXU_N, dtype=jnp.int32)[:, None]
    dst = jnp.arange(MXU_N, dtype=jnp.int32)[None, :]
    half = MXU_N // 2
    perm = jnp.where(dst < half, src == 2 * dst, src == 2 * (dst - half) + 1).astype(BF16)

    return pl.pallas_call(
        _expert_kernel,
        out_shape=jax.ShapeDtypeStruct((N_EROWS, D), F32),
        grid_spec=pltpu.PrefetchScalarGridSpec(
            num_scalar_prefetch=2,
            grid=(N_EBLK,),
            in_specs=[pl.BlockSpec((EBLK, D), row_map),
                      pl.BlockSpec((1, D, 2 * D_EXPERT), w_map),
                      pl.BlockSpec((1, 1, D_EXPERT), w_map),
                      pl.BlockSpec((1, 1, D_EXPERT), w_map),
                      pl.BlockSpec((1, D_EXPERT, D), w_map),
                      pl.BlockSpec((1, 1, D), w_map),
                      pl.BlockSpec((MXU_N, MXU_N), lambda b, be, na: (0, 0))],
            out_specs=pl.BlockSpec((EBLK, D), lambda b, be, na: (b, 0)),
            scratch_shapes=[pltpu.VMEM((D, D_EXPERT), BF16), pltpu.VMEM((D, D_EXPERT), BF16),
                            pltpu.VMEM((D_EXPERT, D), BF16)],
        ),
        compiler_params=pltpu.CompilerParams(dimension_semantics=("arbitrary",),
                                             vmem_limit_bytes=52 * 1024 * 1024),
        name="moe_experts",
    )(block_e, n_active, xb, w_up, bg, bl, w_down, bd, perm)


def _combine_kernel(dest_ref, h1_ref, route_ref, g2_ref, lng_ref, lnb_ref, yb_hbm, o_ref, gbuf, sem):
    i = pl.program_id(0)
    slot = i % 2

    def issue(tile, buf):
        def body(r, carry):
            for k in range(TOP_K):
                d = dest_ref[(tile * TOK_TILE + r) * TOP_K + k]
                pltpu.make_async_copy(yb_hbm.at[pl.ds(d, 1)], gbuf.at[buf, k, pl.ds(r, 1)],
                                      sem.at[buf]).start(priority=k % 2)
            return carry

        lax.fori_loop(0, TOK_TILE, body, 0, unroll=8)

    @pl.when(i == 0)
    def _():
        issue(0, 0)

    for k in range(TOP_K):
        pltpu.make_async_copy(yb_hbm.at[pl.ds(0, TOK_TILE)], gbuf.at[slot, k], sem.at[slot]).wait()

    @pl.when(i + 1 < pl.num_programs(0))
    def _():
        issue(i + 1, 1 - slot)

    route = route_ref[...]
    y = route[:, TOP_K:TOP_K + 1] * gbuf[slot, 0]
    for k in range(1, TOP_K):
        y = y + route[:, TOP_K + k:TOP_K + k + 1] * gbuf[slot, k]
    h = DEEPNORM_ALPHA * h1_ref[...] + _per_batch(y, g2_ref[...])
    h2 = _layer_norm(h) * lng_ref[...] + lnb_ref[...]
    o_ref[...] = pltpu.einshape("sbd->bsd", h2.reshape(TOK_TILE // BATCH, BATCH, D))


def _combine(dest, h1, route, g2, ln2g, ln2b, yb):
    row_spec = pl.BlockSpec((TOK_TILE, D), lambda i, *_: (i, 0))
    vec_spec = pl.BlockSpec((1, D), lambda i, *_: (0, 0))
    return pl.pallas_call(
        _combine_kernel,
        out_shape=jax.ShapeDtypeStruct((BATCH, SEQ, D), F32),
        grid_spec=pltpu.PrefetchScalarGridSpec(
            num_scalar_prefetch=1,
            grid=(T // TOK_TILE,),
            in_specs=[row_spec,
                      pl.BlockSpec((TOK_TILE, ROUTE_W), lambda i, *_: (i, 0)),
                      pl.BlockSpec((BATCH, D), lambda i, *_: (0, 0)),
                      vec_spec, vec_spec,
                      pl.BlockSpec(memory_space=pl.ANY)],
            out_specs=pl.BlockSpec((BATCH, TOK_TILE // BATCH, D), lambda i, *_: (0, i, 0)),
            scratch_shapes=[pltpu.VMEM((2, TOP_K, TOK_TILE, D), F32), pltpu.SemaphoreType.DMA((2,))],
        ),
        compiler_params=pltpu.CompilerParams(dimension_semantics=("arbitrary",),
                                             vmem_limit_bytes=32 * 1024 * 1024),
        name="moe_combine",
    )(dest, h1, route, g2, ln2g, ln2b, yb)


def _routing_plan(route, counts):
    top_idx = route[:, :TOP_K].astype(jnp.int32)
    slot = route[:, 2 * TOP_K:3 * TOP_K].astype(jnp.int32)
    counts = counts[0, :N_EXPERTS].astype(jnp.int32)
    padded = (counts + EBLK - 1) // EBLK * EBLK
    pend = jnp.cumsum(padded)
    pstart = pend - padded
    dest = pstart[top_idx] + slot
    n_active = (pend[-1] // EBLK).astype(jnp.int32).reshape(1)
    blk_start = jnp.arange(N_EBLK, dtype=jnp.int32) * EBLK
    block_e = jnp.minimum((pend[None, :] <= blk_start[:, None]).astype(jnp.int32).sum(1), N_EXPERTS - 1)
    zrow = jnp.maximum(pend - EBLK, 0).astype(jnp.int32)
    zflag = (padded > 0).astype(jnp.int32)
    return dest.reshape(-1).astype(jnp.int32), block_e.astype(jnp.int32), n_active, zrow, zflag


def kernel(x, c, w_ada, b_ada, w_in, b_in, w_cv_dw, b_cv_dw, ln_cv_g, ln_cv_b, w_cv_out, w_lru_conv, b_lru_conv, w_lru_a, b_lru_a, w_lru_x, b_lru_x, lru_lambda, w_lru_out, w_o, b_o, ln1_g, ln1_b, w_router, b_router, w_up, b_up, w_down, b_down, ln2_g, ln2_b):
    def vec(v):
        return v.reshape(1, -1)

    mod = _modulation(c, w_ada[0], b_ada[0])

    w_r = jnp.pad(w_router[0], ((0, 0), (0, ROUTE_W - N_EXPERTS)))
    w_r_hi = w_r.astype(BF16)
    w_r = jnp.concatenate([w_r_hi, (w_r - w_r_hi.astype(F32)).astype(BF16)], axis=1)
    b_r = jnp.pad(b_router[0], (0, ROUTE_W - N_EXPERTS), constant_values=NEG_BIG)
    h1, u2, route, counts = _mixer(
        x, mod, w_in[0].astype(BF16), vec(b_in[0]), w_cv_dw[0], vec(b_cv_dw[0]), vec(ln_cv_g[0]),
        vec(ln_cv_b[0]), w_cv_out[0].astype(BF16), w_lru_conv[0], vec(b_lru_conv[0]),
        w_lru_a[0].astype(BF16), vec(b_lru_a[0]), w_lru_x[0].astype(BF16), vec(b_lru_x[0]),
        vec(lru_lambda[0]), w_lru_out[0].astype(BF16), w_o[0].astype(BF16), vec(b_o[0]),
        vec(ln1_g[0]), vec(ln1_b[0]), w_r, vec(b_r))

    dest, block_e, n_active, zrow, zflag = _routing_plan(route, counts)
    xb = _dispatch(dest, zrow, zflag, n_active, u2)

    bg = b_up[0][:, 0::2].reshape(N_EXPERTS, 1, D_EXPERT)
    bl = b_up[0][:, 1::2].reshape(N_EXPERTS, 1, D_EXPERT)
    bd = b_down[0].reshape(N_EXPERTS, 1, D)
    yb = _experts(block_e, n_active, xb, w_up[0], bg, bl, w_down[0], bd)

    out = _combine(dest, h1, route, mod[:, 5 * D:], vec(ln2_g[0]), vec(ln2_b[0]), yb)
    return out
```

```python
import functools

import jax
import jax.numpy as jnp
from jax import lax
from jax.experimental import pallas as pl
from jax.experimental.pallas import tpu as pltpu

F32 = jnp.float32
BF16 = jnp.bfloat16
HIGHEST = lax.Precision.HIGHEST

D = 1024
BATCH = 8
SEQ = 2048
T = BATCH * SEQ
CONV_K = 31
LRU_CONV_K = 4
LRU_HEADS = 8
LRU_HEAD_DIM = D // LRU_HEADS
LRU_C = 8.0
N_EXPERTS = 32
TOP_K = 4
D_EXPERT = D
SWIGLU_LIMIT = 7.0
SWIGLU_ALPHA = 1.702
LN_EPS = 1e-5
DEEPNORM_ALPHA = 2.0 ** 0.25
IN_COLS = 6 * D

LANES = 128
SUBLANES = 8
MXU_N = 256

TS = 64
ROWS = TS * BATCH
CV_HIST = (CONV_K - 1) * BATCH
LRU_HIST = (LRU_CONV_K - 1) * BATCH
CONV_CHUNK = 64

EBLK = 512
N_EBLK = (T * TOP_K + N_EXPERTS * (EBLK - 1) + EBLK - 1) // EBLK
N_EROWS = N_EBLK * EBLK
PAD_ROWS = 2 * EBLK
TOK_TILE = 256

ROUTE_W = LANES
NEG_BIG = -1e30


def _layer_norm(x):
    mu = jnp.mean(x, axis=-1, keepdims=True)
    xc = x - mu
    var = jnp.mean(xc * xc, axis=-1, keepdims=True)
    return xc * lax.rsqrt(var + LN_EPS)


def _per_batch(v, scale=None, shift=None):
    rows = v.shape[0]
    v3 = v.reshape(rows // BATCH, BATCH, D)
    if scale is not None:
        v3 = v3 * scale[None]
    if shift is not None:
        v3 = v3 + shift[None]
    return v3.reshape(rows, D)


def _bdot(a, w):
    return jnp.dot(a.astype(BF16), w, preferred_element_type=F32)


def _mod_kernel(c_ref, w_ref, b_ref, o_ref):
    c = c_ref[...]
    o_ref[...] = jnp.dot(c * jax.nn.sigmoid(c), w_ref[...], preferred_element_type=F32,
                         precision=HIGHEST) + b_ref[...]


def _modulation(c, w_ada, b_ada):
    tn = 1024
    return pl.pallas_call(
        _mod_kernel,
        out_shape=jax.ShapeDtypeStruct((BATCH, 6 * D), F32),
        grid=(6 * D // tn,),
        in_specs=[pl.BlockSpec((BATCH, D), lambda j: (0, 0)),
                  pl.BlockSpec((D, tn), lambda j: (0, j)),
                  pl.BlockSpec((1, tn), lambda j: (0, j))],
        out_specs=pl.BlockSpec((BATCH, tn), lambda j: (0, j)),
        name="adaln_mod",
    )(c, w_ada, b_ada.reshape(1, 6 * D))


def _mixer_kernel(x_ref, mod_ref, w_in_ref, b_in_ref, w_cvdw_ref, b_cvdw_ref, lncv_g_ref, lncv_b_ref,
                  w_cvout_ref, w_lconv_ref, b_lconv_ref, w_la_ref, b_la_ref, w_lx_ref, b_lx_ref,
                  lam_ref, w_lout_ref, w_o_ref, b_o_ref, ln1g_ref, ln1b_ref, w_r_ref, b_r_ref,
                  h1_ref, u2_ref, route_ref, counts_ref,
                  cvbuf, cvo_s, lbuf, a_s, u_s, hs_s, hstate, cnt_s):
    i = pl.program_id(0)

    @pl.when(i == 0)
    def _():
        cvbuf[0:CV_HIST, :] = jnp.zeros((CV_HIST, D), F32)
        lbuf[0:LRU_HIST, :] = jnp.zeros((LRU_HIST, D), F32)
        hstate[...] = jnp.zeros((BATCH, D), F32)
        cnt_s[...] = jnp.zeros((1, ROUTE_W), F32)

    def mod(j):
        return mod_ref[:, j * D:(j + 1) * D]

    x = jnp.swapaxes(x_ref[...], 0, 1).reshape(ROWS, D)
    u = _per_batch(_layer_norm(x), 1.0 + mod(1), mod(0))
    ub = u.astype(BF16)

    def in_proj(lo, hi):
        return jnp.dot(ub, w_in_ref[:, lo:hi], preferred_element_type=F32) + b_in_ref[:, lo:hi]

    zc = in_proj(0, 2 * D)
    cvbuf[CV_HIST:CV_HIST + ROWS, :] = zc[:, :D] * jax.nn.sigmoid(zc[:, D:])
    lbuf[LRU_HIST:LRU_HIST + ROWS, :] = in_proj(2 * D, 3 * D)
    zg = in_proj(3 * D, 4 * D)
    zm = in_proj(4 * D, 6 * D)

    for rc in range(ROWS // CONV_CHUNK):
        r0 = rc * CONV_CHUNK
        for c in range(D // LANES):
            ls = slice(c * LANES, (c + 1) * LANES)
            acc = jnp.broadcast_to(b_cvdw_ref[:, ls], (CONV_CHUNK, LANES))
            for k in range(CONV_K):
                acc = acc + w_cvdw_ref[k:k + 1, ls] * cvbuf[r0 + BATCH * k:r0 + BATCH * k + CONV_CHUNK, ls]
            cvo_s[r0:r0 + CONV_CHUNK, ls] = acc
    cvbuf[0:CV_HIST, :] = cvbuf[ROWS:ROWS + CV_HIST, :]

    ya = _layer_norm(cvo_s[...]) * lncv_g_ref[...] + lncv_b_ref[...]
    ya = ya * jax.nn.sigmoid(ya)
    ya = _bdot(ya, w_cvout_ref[...])

    xl = jnp.broadcast_to(b_lconv_ref[...], (ROWS, D))
    for k in range(LRU_CONV_K):
        xl = xl + w_lconv_ref[k:k + 1, :] * lbuf[BATCH * k:BATCH * k + ROWS, :]
    lbuf[0:LRU_HIST, :] = lbuf[ROWS:ROWS + LRU_HIST, :]

    ga, gx = [], []
    for h in range(LRU_HEADS):
        xh = xl[:, h * LRU_HEAD_DIM:(h + 1) * LRU_HEAD_DIM].astype(BF16)
        ga.append(jnp.dot(xh, w_la_ref[h], preferred_element_type=F32))
        gx.append(jnp.dot(xh, w_lx_ref[h], preferred_element_type=F32))
    gate_a = jax.nn.sigmoid(jnp.concatenate(ga, axis=-1) + b_la_ref[...])
    gate_x = jax.nn.sigmoid(jnp.concatenate(gx, axis=-1) + b_lx_ref[...])
    nlam = -lam_ref[...]
    softplus = jnp.maximum(nlam, 0.0) + jnp.log1p(jnp.exp(-jnp.abs(nlam)))
    log_a = (-LRU_C * gate_a) * softplus
    a = jnp.exp(log_a)
    mult = jnp.sqrt(1.0 - a * a)
    row = lax.broadcasted_iota(jnp.int32, (ROWS, D), 0) + i * ROWS
    mult = jnp.where(row < BATCH, 1.0, mult)
    a_s[...] = a
    u_s[...] = mult * (gate_x * xl)

    h = hstate[...]
    for t in range(TS):
        h = a_s[t * BATCH:(t + 1) * BATCH, :] * h + u_s[t * BATCH:(t + 1) * BATCH, :]
        hs_s[t * BATCH:(t + 1) * BATCH, :] = h
    hstate[...] = h

    gelu = 0.5 * zg * (1.0 + jnp.tanh(0.7978845608028654 * (zg + 0.044715 * (zg * zg * zg))))
    yb = _bdot(hs_s[...] * gelu, w_lout_ref[...])

    gm = jax.nn.sigmoid(zm)
    mix = _bdot(gm[:, :D] * ya + gm[:, D:] * yb, w_o_ref[...]) + b_o_ref[...]
    h1 = _layer_norm(DEEPNORM_ALPHA * x + _per_batch(mix, mod(2))) * ln1g_ref[...] + ln1b_ref[...]
    h1_ref[...] = h1

    u2 = _per_batch(_layer_norm(h1), 1.0 + mod(4), mod(3))
    u2_ref[...] = u2
    u2_hi = u2.astype(BF16)
    u2_lo = (u2 - u2_hi.astype(F32)).astype(BF16)
    p_hi = jnp.dot(u2_hi, w_r_ref[...], preferred_element_type=F32)
    p_lo = jnp.dot(u2_lo, w_r_ref[...], preferred_element_type=F32)
    logits = (p_hi[:, :ROUTE_W] + (p_hi[:, ROUTE_W:] + p_lo[:, :ROUTE_W] + p_lo[:, ROUTE_W:])) + b_r_ref[...]
    lane = lax.broadcasted_iota(jnp.int32, (ROWS, ROUTE_W), 1)
    vals, idxs = [], []
    for _ in range(TOP_K):
        m = jnp.max(logits, axis=-1, keepdims=True)
        idx = jnp.min(jnp.where(logits == m, lane, ROUTE_W), axis=-1, keepdims=True)
        vals.append(m)
        idxs.append(idx)
        logits = jnp.where(lane == idx, -jnp.inf, logits)
    exps = [jnp.exp(v - vals[0]) for v in vals]
    denom = exps[0] + exps[1] + exps[2] + exps[3]
    route = jnp.zeros((ROWS, ROUTE_W), F32)
    for k in range(TOP_K):
        route = jnp.where(lane == k, idxs[k].astype(F32), route)
        route = jnp.where(lane == TOP_K + k, exps[k] / denom, route)

    sel = jnp.zeros((ROWS, ROUTE_W), F32)
    for k in range(TOP_K):
        sel = jnp.where(lane == idxs[k], 1.0, sel)
    tri = (lax.broadcasted_iota(jnp.int32, (ROWS, ROWS), 0)
           >= lax.broadcasted_iota(jnp.int32, (ROWS, ROWS), 1)).astype(BF16)
    seen = jnp.dot(tri, sel.astype(BF16), preferred_element_type=F32) + cnt_s[...]
    for k in range(TOP_K):
        slot = jnp.sum(jnp.where(lane == idxs[k], seen - 1.0, 0.0), axis=-1, keepdims=True)
        route = jnp.where(lane == 2 * TOP_K + k, slot, route)
    route_ref[...] = route
    cnt_s[...] = seen[ROWS - 1:ROWS, :]
    counts_ref[...] = jnp.broadcast_to(seen[ROWS - 1:ROWS, :], (SUBLANES, ROUTE_W))


def _const_spec(shape):
    nd = len(shape)
    return pl.BlockSpec(shape, lambda i: (0,) * nd, pipeline_mode=pl.Buffered(1))


def _mixer(xt, mod, w_in, b_in, w_cvdw, b_cvdw, lncv_g, lncv_b, w_cvout, w_lconv, b_lconv,
           w_la, b_la, w_lx, b_lx, lam, w_lout, w_o, b_o, ln1g, ln1b, w_r, b_r):
    consts = (mod, w_in, b_in, w_cvdw, b_cvdw, lncv_g, lncv_b, w_cvout, w_lconv, b_lconv,
              w_la, b_la, w_lx, b_lx, lam, w_lout, w_o, b_o, ln1g, ln1b, w_r, b_r)
    row_spec = pl.BlockSpec((ROWS, D), lambda i: (i, 0))
    return pl.pallas_call(
        _mixer_kernel,
        out_shape=(jax.ShapeDtypeStruct((T, D), F32), jax.ShapeDtypeStruct((T, D), F32),
                   jax.ShapeDtypeStruct((T, ROUTE_W), F32),
                   jax.ShapeDtypeStruct((SUBLANES, ROUTE_W), F32)),
        grid=(SEQ // TS,),
        in_specs=[pl.BlockSpec((BATCH, TS, D), lambda i: (0, i, 0))] + [_const_spec(a.shape) for a in consts],
        out_specs=(row_spec, row_spec, pl.BlockSpec((ROWS, ROUTE_W), lambda i: (i, 0)),
                   pl.BlockSpec((SUBLANES, ROUTE_W), lambda i: (0, 0))),
        scratch_shapes=[pltpu.VMEM((CV_HIST + ROWS, D), F32),
                        pltpu.VMEM((ROWS, D), F32),
                        pltpu.VMEM((LRU_HIST + ROWS, D), F32),
                        pltpu.VMEM((ROWS, D), F32),
                        pltpu.VMEM((ROWS, D), F32),
                        pltpu.VMEM((ROWS, D), F32),
                        pltpu.VMEM((BATCH, D), F32),
                        pltpu.VMEM((1, ROUTE_W), F32)],
        compiler_params=pltpu.CompilerParams(dimension_semantics=("arbitrary",),
                                             vmem_limit_bytes=58 * 1024 * 1024),
        name="mixer_router",
    )(xt, *consts)


def _expert_kernel(be_ref, na_ref, sd_ref, u_hbm, wup_ref, bg_ref, bl_ref, wdn_ref, bd_ref, perm_ref, y_hbm,
                   xbuf, obuf, wg_s, wl_s, wd_s, gsem, ssem):
    b = pl.program_id(0)
    na = na_ref[0]
    active = b < na
    slot = b % 2
    new_expert = jnp.logical_or(b == 0, be_ref[b] != be_ref[jnp.maximum(b - 1, 0)])

    def start_gather(blk, buf):
        for r in range(EBLK):
            tok = jnp.bitwise_and(sd_ref[blk * EBLK + r], T - 1)
            pltpu.make_async_copy(u_hbm.at[pl.ds(tok, 1)], xbuf.at[buf, pl.ds(r, 1)],
                                  gsem.at[buf]).start(priority=r % 2)

    def wait_gather(buf):
        pltpu.make_async_copy(u_hbm.at[pl.ds(0, EBLK)], xbuf.at[buf], gsem.at[buf]).wait()

    def start_scatter(blk, buf):
        for r in range(EBLK):
            pltpu.make_async_copy(obuf.at[buf, pl.ds(r, 1)], y_hbm.at[pl.ds(sd_ref[blk * EBLK + r], 1)],
                                  ssem.at[buf]).start(priority=r % 2)

    def wait_scatter(buf):
        pltpu.make_async_copy(obuf.at[buf], y_hbm.at[pl.ds(0, EBLK)], ssem.at[buf]).wait()

    def compute(buf):
        xb = xbuf[buf].astype(BF16)
        zg = jnp.dot(xb, wg_s[...], preferred_element_type=F32) + bg_ref[0]
        zl = jnp.dot(xb, wl_s[...], preferred_element_type=F32) + bl_ref[0]
        zg = jnp.minimum(zg, SWIGLU_LIMIT)
        zl = jnp.clip(zl, -SWIGLU_LIMIT, SWIGLU_LIMIT)
        act = zg * jax.nn.sigmoid(SWIGLU_ALPHA * zg) * (zl + 1.0)
        obuf[buf] = jnp.dot(act.astype(BF16), wd_s[...], preferred_element_type=F32) + bd_ref[0]

    @pl.when(b == 0)
    def _():
        start_gather(0, 0)
        obuf[1] = jnp.zeros((EBLK, D), F32)
        clears = [pltpu.make_async_copy(obuf.at[1], y_hbm.at[pl.ds(T * TOP_K + j * EBLK, EBLK)], ssem.at[1])
                  for j in range(PAD_ROWS // EBLK)]
        for cp in clears:
            cp.start()
        for cp in clears:
            cp.wait()

    @pl.when(jnp.logical_and(active, new_expert))
    def _():
        for s in range(2 * D_EXPERT // MXU_N):
            slab = wup_ref[0, :, s * MXU_N:(s + 1) * MXU_N].astype(BF16)
            split = jnp.dot(slab, perm_ref[...], preferred_element_type=F32).astype(BF16)
            half = MXU_N // 2
            wg_s[:, s * half:(s + 1) * half] = split[:, :half]
            wl_s[:, s * half:(s + 1) * half] = split[:, half:]
        wd_s[...] = wdn_ref[0].astype(BF16)

    @pl.when(active)
    def _():
        wait_gather(slot)

        @pl.when(b >= 2)
        def _():
            wait_scatter(slot)

    nxt = jnp.minimum(b + 1, N_EBLK - 1)

    @pl.when(b == 0)
    def _():
        start_gather(nxt, 1)
        compute(0)

    for buf in range(2):
        @pl.when(jnp.logical_and(active, jnp.logical_and(slot == buf, b > 0)))
        def _():
            start_gather(nxt, 1 - buf)
            start_scatter(b - 1, 1 - buf)
            compute(buf)

    def finalize(last, buf_last):
        wait_gather(1 - buf_last)
        start_scatter(last, buf_last)
        wait_scatter(buf_last)

        @pl.when(last >= 1)
        def _():
            wait_scatter(1 - buf_last)

    for buf in range(2):
        @pl.when(jnp.logical_and(b == na, (na - 1) % 2 == buf))
        def _():
            finalize(na - 1, buf)

        @pl.when(jnp.logical_and(jnp.logical_and(b == N_EBLK - 1, na == N_EBLK), slot == buf))
        def _():
            finalize(b, buf)


def _experts(block_e, n_active, slot_dst, u2, w_up, bg, bl, w_down, bd):
    def w_map(b, be, na, sd):
        return (be[b], 0, 0)

    src = jnp.arange(MXU_N, dtype=jnp.int32)[:, None]
    dst = jnp.arange(MXU_N, dtype=jnp.int32)[None, :]
    half = MXU_N // 2
    perm = jnp.where(dst < half, src == 2 * dst, src == 2 * (dst - half) + 1).astype(BF16)

    return pl.pallas_call(
        _expert_kernel,
        out_shape=jax.ShapeDtypeStruct((T * TOP_K + PAD_ROWS, D), F32),
        grid_spec=pltpu.PrefetchScalarGridSpec(
            num_scalar_prefetch=3,
            grid=(N_EBLK,),
            in_specs=[pl.BlockSpec(memory_space=pl.ANY),
                      pl.BlockSpec((1, D, 2 * D_EXPERT), w_map),
                      pl.BlockSpec((1, 1, D_EXPERT), w_map),
                      pl.BlockSpec((1, 1, D_EXPERT), w_map),
                      pl.BlockSpec((1, D_EXPERT, D), w_map),
                      pl.BlockSpec((1, 1, D), w_map),
                      pl.BlockSpec((MXU_N, MXU_N), lambda b, be, na, sd: (0, 0))],
            out_specs=pl.BlockSpec(memory_space=pl.ANY),
            scratch_shapes=[pltpu.VMEM((2, EBLK, D), F32), pltpu.VMEM((2, EBLK, D), F32),
                            pltpu.VMEM((D, D_EXPERT), BF16), pltpu.VMEM((D, D_EXPERT), BF16),
                            pltpu.VMEM((D_EXPERT, D), BF16),
                            pltpu.SemaphoreType.DMA((2,)), pltpu.SemaphoreType.DMA((2,))],
        ),
        compiler_params=pltpu.CompilerParams(dimension_semantics=("arbitrary",),
                                             vmem_limit_bytes=56 * 1024 * 1024),
        name="moe_experts",
    )(block_e, n_active, slot_dst, u2, w_up, bg, bl, w_down, bd, perm)


def _combine_kernel(h1_ref, route_ref, g2_ref, lng_ref, lnb_ref, y0_ref, y1_ref, y2_ref, y3_ref, o_ref):
    route = route_ref[...]
    y = route[:, TOP_K:TOP_K + 1] * y0_ref[...]
    for k, y_ref in ((1, y1_ref), (2, y2_ref), (3, y3_ref)):
        y = y + route[:, TOP_K + k:TOP_K + k + 1] * y_ref[...]
    h = DEEPNORM_ALPHA * h1_ref[...] + _per_batch(y, g2_ref[...])
    h2 = _layer_norm(h) * lng_ref[...] + lnb_ref[...]
    o_ref[...] = jnp.swapaxes(h2.reshape(TOK_TILE // BATCH, BATCH, D), 0, 1)


def _combine(h1, route, g2, ln2g, ln2b, y4):
    row_spec = pl.BlockSpec((TOK_TILE, D), lambda i: (i, 0))
    vec_spec = pl.BlockSpec((1, D), lambda i: (0, 0))
    return pl.pallas_call(
        _combine_kernel,
        out_shape=jax.ShapeDtypeStruct((BATCH, SEQ, D), F32),
        grid=(T // TOK_TILE,),
        in_specs=[row_spec,
                  pl.BlockSpec((TOK_TILE, ROUTE_W), lambda i: (i, 0)),
                  pl.BlockSpec((BATCH, D), lambda i: (0, 0)),
                  vec_spec, vec_spec]
                 + [pl.BlockSpec((TOK_TILE, D), functools.partial(lambda k, i: (k * (T // TOK_TILE) + i, 0), k))
                    for k in range(TOP_K)],
        out_specs=pl.BlockSpec((BATCH, TOK_TILE // BATCH, D), lambda i: (0, i, 0)),
        compiler_params=pltpu.CompilerParams(dimension_semantics=("arbitrary",),
                                             vmem_limit_bytes=32 * 1024 * 1024),
        name="moe_combine",
    )(h1, route, g2, ln2g, ln2b, y4, y4, y4, y4)


def _routing_plan(route, counts):
    top_idx = route[:, :TOP_K].astype(jnp.int32)
    slot = route[:, 2 * TOP_K:3 * TOP_K].astype(jnp.int32)
    counts = counts[0, :N_EXPERTS].astype(jnp.int32)
    padded = (counts + EBLK - 1) // EBLK * EBLK
    pend = jnp.cumsum(padded)
    pstart = pend - padded
    dest = (pstart[top_idx] + slot).reshape(-1)
    n_active = (pend[-1] // EBLK).astype(jnp.int32).reshape(1)
    blk_start = jnp.arange(N_EBLK, dtype=jnp.int32) * EBLK
    block_e = jnp.minimum((pend[None, :] <= blk_start[:, None]).astype(jnp.int32).sum(1), N_EXPERTS - 1)
    pad_dst = T * TOP_K + jnp.arange(N_EROWS, dtype=jnp.int32) % PAD_ROWS
    real_dst = (jnp.arange(TOP_K, dtype=jnp.int32)[None, :] * T
                + jnp.arange(T, dtype=jnp.int32)[:, None]).reshape(-1)
    slot_dst = pad_dst.at[dest].set(real_dst, unique_indices=True)
    return slot_dst, block_e.astype(jnp.int32), n_active


def kernel(x, c, w_ada, b_ada, w_in, b_in, w_cv_dw, b_cv_dw, ln_cv_g, ln_cv_b, w_cv_out, w_lru_conv, b_lru_conv, w_lru_a, b_lru_a, w_lru_x, b_lru_x, lru_lambda, w_lru_out, w_o, b_o, ln1_g, ln1_b, w_router, b_router, w_up, b_up, w_down, b_down, ln2_g, ln2_b):
    def vec(v):
        return v.reshape(1, -1)

    mod = _modulation(c, w_ada[0], b_ada[0])

    w_r = jnp.pad(w_router[0], ((0, 0), (0, ROUTE_W - N_EXPERTS)))
    w_r_hi = w_r.astype(BF16)
    w_r = jnp.concatenate([w_r_hi, (w_r - w_r_hi.astype(F32)).astype(BF16)], axis=1)
    b_r = jnp.pad(b_router[0], (0, ROUTE_W - N_EXPERTS), constant_values=NEG_BIG)
    h1, u2, route, counts = _mixer(
        x, mod, w_in[0].astype(BF16), vec(b_in[0]), w_cv_dw[0], vec(b_cv_dw[0]), vec(ln_cv_g[0]),
        vec(ln_cv_b[0]), w_cv_out[0].astype(BF16), w_lru_conv[0], vec(b_lru_conv[0]),
        w_lru_a[0].astype(BF16), vec(b_lru_a[0]), w_lru_x[0].astype(BF16), vec(b_lru_x[0]),
        vec(lru_lambda[0]), w_lru_out[0].astype(BF16), w_o[0].astype(BF16), vec(b_o[0]),
        vec(ln1_g[0]), vec(ln1_b[0]), w_r, vec(b_r))

    slot_dst, block_e, n_active = _routing_plan(route, counts)

    bg = b_up[0][:, 0::2].reshape(N_EXPERTS, 1, D_EXPERT)
    bl = b_up[0][:, 1::2].reshape(N_EXPERTS, 1, D_EXPERT)
    bd = b_down[0].reshape(N_EXPERTS, 1, D)
    y4 = _experts(block_e, n_active, slot_dst, u2, w_up[0], bg, bl, w_down[0], bd)

    return _combine(h1, route, mod[:, 5 * D:], vec(ln2_g[0]), vec(ln2_b[0]), y4)
```

```python
import functools

import jax
import jax.numpy as jnp
from jax import lax
from jax.experimental import pallas as pl
from jax.experimental.pallas import tpu as pltpu

F32 = jnp.float32
BF16 = jnp.bfloat16
HIGHEST = lax.Precision.HIGHEST

D = 1024
BATCH = 8
SEQ = 2048
T = BATCH * SEQ
CONV_K = 31
LRU_CONV_K = 4
LRU_HEADS = 8
LRU_HEAD_DIM = D // LRU_HEADS
LRU_C = 8.0
N_EXPERTS = 32
TOP_K = 4
D_EXPERT = D
SWIGLU_LIMIT = 7.0
SWIGLU_ALPHA = 1.702
LN_EPS = 1e-5
DEEPNORM_ALPHA = 2.0 ** 0.25
IN_COLS = 6 * D

LANES = 128
SUBLANES = 8
MXU_N = 256
assert D == SUBLANES * LANES

TS = 64
ROWS = TS * BATCH
CV_HIST = (CONV_K - 1) * BATCH
LRU_HIST = (LRU_CONV_K - 1) * BATCH
CONV_CHUNK = 64

EBLK = 512
N_EBLK = (T * TOP_K + N_EXPERTS * (EBLK - 1) + EBLK - 1) // EBLK
N_EROWS = N_EBLK * EBLK
TOK_TILE = 256

ROUTE_W = LANES
NEG_BIG = -1e30


def _layer_norm(x):
    mu = jnp.mean(x, axis=-1, keepdims=True)
    xc = x - mu
    var = jnp.mean(xc * xc, axis=-1, keepdims=True)
    return xc * lax.rsqrt(var + LN_EPS)


def _per_batch(v, scale=None, shift=None):
    rows = v.shape[0]
    v3 = v.reshape(rows // BATCH, BATCH, D)
    if scale is not None:
        v3 = v3 * scale[None]
    if shift is not None:
        v3 = v3 + shift[None]
    return v3.reshape(rows, D)


def _bdot(a, w):
    return jnp.dot(a.astype(BF16), w, preferred_element_type=F32)


def _to_tiles(x):
    rows = x.shape[0]
    cols = [x[:, s * LANES:(s + 1) * LANES].reshape(rows // SUBLANES, SUBLANES, LANES) for s in range(SUBLANES)]
    return jnp.swapaxes(jnp.stack(cols, axis=1), 1, 2).reshape(rows, SUBLANES, LANES)


def _from_tiles(g):
    rows = g.shape[0]
    h = jnp.swapaxes(g.reshape(rows // SUBLANES, SUBLANES, SUBLANES, LANES), 1, 2)
    return jnp.concatenate([h[:, s].reshape(rows, LANES) for s in range(SUBLANES)], axis=-1)


def _mod_kernel(c_ref, w_ref, b_ref, o_ref):
    c = c_ref[...]
    o_ref[...] = jnp.dot(c * jax.nn.sigmoid(c), w_ref[...], preferred_element_type=F32,
                         precision=HIGHEST) + b_ref[...]


def _modulation(c, w_ada, b_ada):
    tn = 1024
    return pl.pallas_call(
        _mod_kernel,
        out_shape=jax.ShapeDtypeStruct((BATCH, 6 * D), F32),
        grid=(6 * D // tn,),
        in_specs=[pl.BlockSpec((BATCH, D), lambda j: (0, 0)),
                  pl.BlockSpec((D, tn), lambda j: (0, j)),
                  pl.BlockSpec((1, tn), lambda j: (0, j))],
        out_specs=pl.BlockSpec((BATCH, tn), lambda j: (0, j)),
        name="adaln_mod",
    )(c, w_ada, b_ada.reshape(1, 6 * D))


def _mixer_kernel(x_ref, mod_ref, w_in_ref, b_in_ref, w_cvdw_ref, b_cvdw_ref, lncv_g_ref, lncv_b_ref,
                  w_cvout_ref, w_lconv_ref, b_lconv_ref, w_la_ref, b_la_ref, w_lx_ref, b_lx_ref,
                  lam_ref, w_lout_ref, w_o_ref, b_o_ref, ln1g_ref, ln1b_ref, w_r_ref, b_r_ref,
                  h1_ref, u2_ref, route_ref, counts_ref,
                  cvbuf, cvo_s, lbuf, a_s, u_s, hs_s, hstate, cnt_s):
    i = pl.program_id(0)

    @pl.when(i == 0)
    def _():
        cvbuf[0:CV_HIST, :] = jnp.zeros((CV_HIST, D), F32)
        lbuf[0:LRU_HIST, :] = jnp.zeros((LRU_HIST, D), F32)
        hstate[...] = jnp.zeros((BATCH, D), F32)
        cnt_s[...] = jnp.zeros((1, ROUTE_W), F32)

    def mod(j):
        return mod_ref[:, j * D:(j + 1) * D]

    x = jnp.swapaxes(x_ref[...], 0, 1).reshape(ROWS, D)
    u = _per_batch(_layer_norm(x), 1.0 + mod(1), mod(0))
    ub = u.astype(BF16)

    def in_proj(lo, hi):
        return jnp.dot(ub, w_in_ref[:, lo:hi], preferred_element_type=F32) + b_in_ref[:, lo:hi]

    zc = in_proj(0, 2 * D)
    cvbuf[CV_HIST:CV_HIST + ROWS, :] = zc[:, :D] * jax.nn.sigmoid(zc[:, D:])
    lbuf[LRU_HIST:LRU_HIST + ROWS, :] = in_proj(2 * D, 3 * D)
    zg = in_proj(3 * D, 4 * D)
    zm = in_proj(4 * D, 6 * D)

    for rc in range(ROWS // CONV_CHUNK):
        r0 = rc * CONV_CHUNK
        for c in range(D // LANES):
            ls = slice(c * LANES, (c + 1) * LANES)
            acc = jnp.broadcast_to(b_cvdw_ref[:, ls], (CONV_CHUNK, LANES))
            for k in range(CONV_K):
                acc = acc + w_cvdw_ref[k:k + 1, ls] * cvbuf[r0 + BATCH * k:r0 + BATCH * k + CONV_CHUNK, ls]
            cvo_s[r0:r0 + CONV_CHUNK, ls] = acc
    cvbuf[0:CV_HIST, :] = cvbuf[ROWS:ROWS + CV_HIST, :]

    ya = _layer_norm(cvo_s[...]) * lncv_g_ref[...] + lncv_b_ref[...]
    ya = ya * jax.nn.sigmoid(ya)
    ya = _bdot(ya, w_cvout_ref[...])

    xl = jnp.broadcast_to(b_lconv_ref[...], (ROWS, D))
    for k in range(LRU_CONV_K):
        xl = xl + w_lconv_ref[k:k + 1, :] * lbuf[BATCH * k:BATCH * k + ROWS, :]
    lbuf[0:LRU_HIST, :] = lbuf[ROWS:ROWS + LRU_HIST, :]

    ga, gx = [], []
    for h in range(LRU_HEADS):
        xh = xl[:, h * LRU_HEAD_DIM:(h + 1) * LRU_HEAD_DIM].astype(BF16)
        ga.append(jnp.dot(xh, w_la_ref[h], preferred_element_type=F32))
        gx.append(jnp.dot(xh, w_lx_ref[h], preferred_element_type=F32))
    gate_a = jax.nn.sigmoid(jnp.concatenate(ga, axis=-1) + b_la_ref[...])
    gate_x = jax.nn.sigmoid(jnp.concatenate(gx, axis=-1) + b_lx_ref[...])
    nlam = -lam_ref[...]
    softplus = jnp.maximum(nlam, 0.0) + jnp.log1p(jnp.exp(-jnp.abs(nlam)))
    log_a = (-LRU_C * gate_a) * softplus
    a = jnp.exp(log_a)
    mult = jnp.sqrt(1.0 - a * a)
    row = lax.broadcasted_iota(jnp.int32, (ROWS, D), 0) + i * ROWS
    mult = jnp.where(row < BATCH, 1.0, mult)
    a_s[...] = a
    u_s[...] = mult * (gate_x * xl)

    h = hstate[...]
    for t in range(TS):
        h = a_s[t * BATCH:(t + 1) * BATCH, :] * h + u_s[t * BATCH:(t + 1) * BATCH, :]
        hs_s[t * BATCH:(t + 1) * BATCH, :] = h
    hstate[...] = h

    gelu = 0.5 * zg * (1.0 + jnp.tanh(0.7978845608028654 * (zg + 0.044715 * (zg * zg * zg))))
    yb = _bdot(hs_s[...] * gelu, w_lout_ref[...])

    gm = jax.nn.sigmoid(zm)
    mix = _bdot(gm[:, :D] * ya + gm[:, D:] * yb, w_o_ref[...]) + b_o_ref[...]
    h1 = _layer_norm(DEEPNORM_ALPHA * x + _per_batch(mix, mod(2))) * ln1g_ref[...] + ln1b_ref[...]
    h1_ref[...] = h1

    u2 = _per_batch(_layer_norm(h1), 1.0 + mod(4), mod(3))
    u2_ref[...] = _to_tiles(u2)
    u2_hi = u2.astype(BF16)
    u2_lo = (u2 - u2_hi.astype(F32)).astype(BF16)
    p_hi = jnp.dot(u2_hi, w_r_ref[...], preferred_element_type=F32)
    p_lo = jnp.dot(u2_lo, w_r_ref[...], preferred_element_type=F32)
    logits = (p_hi[:, :ROUTE_W] + (p_hi[:, ROUTE_W:] + p_lo[:, :ROUTE_W] + p_lo[:, ROUTE_W:])) + b_r_ref[...]
    lane = lax.broadcasted_iota(jnp.int32, (ROWS, ROUTE_W), 1)
    vals, idxs = [], []
    for _ in range(TOP_K):
        m = jnp.max(logits, axis=-1, keepdims=True)
        idx = jnp.min(jnp.where(logits == m, lane, ROUTE_W), axis=-1, keepdims=True)
        vals.append(m)
        idxs.append(idx)
        logits = jnp.where(lane == idx, -jnp.inf, logits)
    exps = [jnp.exp(v - vals[0]) for v in vals]
    denom = exps[0] + exps[1] + exps[2] + exps[3]
    route = jnp.zeros((ROWS, ROUTE_W), F32)
    for k in range(TOP_K):
        route = jnp.where(lane == k, idxs[k].astype(F32), route)
        route = jnp.where(lane == TOP_K + k, exps[k] / denom, route)

    sel = jnp.zeros((ROWS, ROUTE_W), F32)
    for k in range(TOP_K):
        sel = jnp.where(lane == idxs[k], 1.0, sel)
    tri = (lax.broadcasted_iota(jnp.int32, (ROWS, ROWS), 0)
           >= lax.broadcasted_iota(jnp.int32, (ROWS, ROWS), 1)).astype(BF16)
    seen = jnp.dot(tri, sel.astype(BF16), preferred_element_type=F32) + cnt_s[...]
    for k in range(TOP_K):
        slot = jnp.sum(jnp.where(lane == idxs[k], seen - 1.0, 0.0), axis=-1, keepdims=True)
        route = jnp.where(lane == 2 * TOP_K + k, slot, route)
    route_ref[...] = route
    cnt_s[...] = seen[ROWS - 1:ROWS, :]
    counts_ref[...] = jnp.broadcast_to(seen[ROWS - 1:ROWS, :], (SUBLANES, ROUTE_W))


def _const_spec(shape):
    nd = len(shape)
    return pl.BlockSpec(shape, lambda i: (0,) * nd, pipeline_mode=pl.Buffered(1))


def _mixer(xt, mod, w_in, b_in, w_cvdw, b_cvdw, lncv_g, lncv_b, w_cvout, w_lconv, b_lconv,
           w_la, b_la, w_lx, b_lx, lam, w_lout, w_o, b_o, ln1g, ln1b, w_r, b_r):
    consts = (mod, w_in, b_in, w_cvdw, b_cvdw, lncv_g, lncv_b, w_cvout, w_lconv, b_lconv,
              w_la, b_la, w_lx, b_lx, lam, w_lout, w_o, b_o, ln1g, ln1b, w_r, b_r)
    row_spec = pl.BlockSpec((ROWS, D), lambda i: (i, 0))
    return pl.pallas_call(
        _mixer_kernel,
        out_shape=(jax.ShapeDtypeStruct((T, D), F32), jax.ShapeDtypeStruct((T, SUBLANES, LANES), F32),
                   jax.ShapeDtypeStruct((T, ROUTE_W), F32),
                   jax.ShapeDtypeStruct((SUBLANES, ROUTE_W), F32)),
        grid=(SEQ // TS,),
        in_specs=[pl.BlockSpec((BATCH, TS, D), lambda i: (0, i, 0))] + [_const_spec(a.shape) for a in consts],
        out_specs=(row_spec, pl.BlockSpec((ROWS, SUBLANES, LANES), lambda i: (i, 0, 0)),
                   pl.BlockSpec((ROWS, ROUTE_W), lambda i: (i, 0)),
                   pl.BlockSpec((SUBLANES, ROUTE_W), lambda i: (0, 0))),
        scratch_shapes=[pltpu.VMEM((CV_HIST + ROWS, D), F32),
                        pltpu.VMEM((ROWS, D), F32),
                        pltpu.VMEM((LRU_HIST + ROWS, D), F32),
                        pltpu.VMEM((ROWS, D), F32),
                        pltpu.VMEM((ROWS, D), F32),
                        pltpu.VMEM((ROWS, D), F32),
                        pltpu.VMEM((BATCH, D), F32),
                        pltpu.VMEM((1, ROUTE_W), F32)],
        compiler_params=pltpu.CompilerParams(dimension_semantics=("arbitrary",),
                                             vmem_limit_bytes=58 * 1024 * 1024),
        name="mixer_router",
    )(xt, *consts)


def _dispatch_kernel(dest_ref, zrow_ref, zflag_ref, na_ref, u_ref, xb_hbm, zbuf, zsem, sem):
    i = pl.program_id(0)

    def zero_block(row):
        return pltpu.make_async_copy(zbuf, xb_hbm.at[pl.ds(pl.multiple_of(row, EBLK), EBLK)], zsem)

    @pl.when(i == 0)
    def _():
        zbuf[...] = jnp.zeros((EBLK, SUBLANES, LANES), F32)
        for e in range(N_EXPERTS):
            @pl.when(zflag_ref[e] > 0)
            def _():
                zero_block(zrow_ref[e]).start()

        def tail_start(b, carry):
            zero_block(b * EBLK).start()
            return carry

        def tail_wait(b, carry):
            zero_block(b * EBLK).wait()
            return carry

        lax.fori_loop(na_ref[0], N_EBLK, tail_start, 0)
        for e in range(N_EXPERTS):
            @pl.when(zflag_ref[e] > 0)
            def _():
                zero_block(zrow_ref[e]).wait()
        lax.fori_loop(na_ref[0], N_EBLK, tail_wait, 0)

    def issue(r, carry):
        for k in range(TOP_K):
            d = dest_ref[(i * TOK_TILE + r) * TOP_K + k]
            pltpu.make_async_copy(u_ref.at[r], xb_hbm.at[d], sem).start(priority=k % 2)
        return carry

    lax.fori_loop(0, TOK_TILE, issue, 0, unroll=8)
    for k in range(TOP_K):
        pltpu.make_async_copy(u_ref, xb_hbm.at[pl.ds(0, TOK_TILE)], sem).wait()


def _dispatch(dest, zrow, zflag, n_active, u2):
    return pl.pallas_call(
        _dispatch_kernel,
        out_shape=jax.ShapeDtypeStruct((N_EROWS, SUBLANES, LANES), F32),
        grid_spec=pltpu.PrefetchScalarGridSpec(
            num_scalar_prefetch=4,
            grid=(T // TOK_TILE,),
            in_specs=[pl.BlockSpec((TOK_TILE, SUBLANES, LANES), lambda i, *_: (i, 0, 0))],
            out_specs=pl.BlockSpec(memory_space=pl.ANY),
            scratch_shapes=[pltpu.VMEM((EBLK, SUBLANES, LANES), F32), pltpu.SemaphoreType.DMA,
                            pltpu.SemaphoreType.DMA],
        ),
        compiler_params=pltpu.CompilerParams(dimension_semantics=("arbitrary",)),
        name="moe_dispatch",
    )(dest, zrow, zflag, n_active, u2)


def _expert_kernel(be_ref, na_ref, x_ref, wup_ref, bg_ref, bl_ref, wdn_ref, bd_ref, perm_ref, o_ref,
                   wg_s, wl_s, wd_s):
    b = pl.program_id(0)
    active = b < na_ref[0]
    new_expert = jnp.logical_or(b == 0, be_ref[b] != be_ref[jnp.maximum(b - 1, 0)])

    @pl.when(jnp.logical_and(active, new_expert))
    def _():
        for s in range(2 * D_EXPERT // MXU_N):
            slab = wup_ref[0, :, s * MXU_N:(s + 1) * MXU_N].astype(BF16)
            split = jnp.dot(slab, perm_ref[...], preferred_element_type=F32).astype(BF16)
            half = MXU_N // 2
            wg_s[:, s * half:(s + 1) * half] = split[:, :half]
            wl_s[:, s * half:(s + 1) * half] = split[:, half:]
        wd_s[...] = wdn_ref[0].astype(BF16)

    @pl.when(active)
    def _():
        xb = _from_tiles(x_ref[...]).astype(BF16)
        zg = jnp.dot(xb, wg_s[...], preferred_element_type=F32) + bg_ref[0]
        zl = jnp.dot(xb, wl_s[...], preferred_element_type=F32) + bl_ref[0]
        zg = jnp.minimum(zg, SWIGLU_LIMIT)
        zl = jnp.clip(zl, -SWIGLU_LIMIT, SWIGLU_LIMIT)
        act = zg * jax.nn.sigmoid(SWIGLU_ALPHA * zg) * (zl + 1.0)
        y = jnp.dot(act.astype(BF16), wd_s[...], preferred_element_type=F32) + bd_ref[0]
        o_ref[...] = _to_tiles(y)

    @pl.when(jnp.logical_not(active))
    def _():
        o_ref[...] = jnp.zeros((EBLK, SUBLANES, LANES), F32)


def _experts(block_e, n_active, xb, w_up, bg, bl, w_down, bd):
    def row_map(b, be, na):
        return (jnp.maximum(jnp.minimum(b, na[0] - 1), 0), 0, 0)

    def w_map(b, be, na):
        return (be[b], 0, 0)

    src = jnp.arange(MXU_N, dtype=jnp.int32)[:, None]
    dst = jnp.arange(MXU_N, dtype=jnp.int32)[None, :]
    half = MXU_N // 2
    perm = jnp.where(dst < half, src == 2 * dst, src == 2 * (dst - half) + 1).astype(BF16)

    return pl.pallas_call(
        _expert_kernel,
        out_shape=jax.ShapeDtypeStruct((N_EROWS, SUBLANES, LANES), F32),
        grid_spec=pltpu.PrefetchScalarGridSpec(
            num_scalar_prefetch=2,
            grid=(N_EBLK,),
            in_specs=[pl.BlockSpec((EBLK, SUBLANES, LANES), row_map),
                      pl.BlockSpec((1, D, 2 * D_EXPERT), w_map),
                      pl.BlockSpec((1, 1, D_EXPERT), w_map),
                      pl.BlockSpec((1, 1, D_EXPERT), w_map),
                      pl.BlockSpec((1, D_EXPERT, D), w_map),
                      pl.BlockSpec((1, 1, D), w_map),
                      pl.BlockSpec((MXU_N, MXU_N), lambda b, be, na: (0, 0))],
            out_specs=pl.BlockSpec((EBLK, SUBLANES, LANES), lambda b, be, na: (b, 0, 0)),
            scratch_shapes=[pltpu.VMEM((D, D_EXPERT), BF16), pltpu.VMEM((D, D_EXPERT), BF16),
                            pltpu.VMEM((D_EXPERT, D), BF16)],
        ),
        compiler_params=pltpu.CompilerParams(dimension_semantics=("arbitrary",),
                                             vmem_limit_bytes=52 * 1024 * 1024),
        name="moe_experts",
    )(block_e, n_active, xb, w_up, bg, bl, w_down, bd, perm)


def _combine_kernel(dest_ref, h1_ref, route_ref, g2_ref, lng_ref, lnb_ref, yb_hbm, o_ref, gbuf, sem):
    i = pl.program_id(0)
    slot = i % 2

    def issue(tile, buf):
        def body(r, carry):
            for k in range(TOP_K):
                d = dest_ref[(tile * TOK_TILE + r) * TOP_K + k]
                pltpu.make_async_copy(yb_hbm.at[d], gbuf.at[buf, k, r], sem.at[buf]).start(priority=k % 2)
            return carry

        lax.fori_loop(0, TOK_TILE, body, 0, unroll=8)

    @pl.when(i == 0)
    def _():
        issue(0, 0)

    for k in range(TOP_K):
        pltpu.make_async_copy(yb_hbm.at[pl.ds(0, TOK_TILE)], gbuf.at[slot, k], sem.at[slot]).wait()

    @pl.when(i + 1 < pl.num_programs(0))
    def _():
        issue(i + 1, 1 - slot)

    route = route_ref[...]
    y = route[:, TOP_K:TOP_K + 1] * _from_tiles(gbuf[slot, 0])
    for k in range(1, TOP_K):
        y = y + route[:, TOP_K + k:TOP_K + k + 1] * _from_tiles(gbuf[slot, k])
    h = DEEPNORM_ALPHA * h1_ref[...] + _per_batch(y, g2_ref[...])
    h2 = _layer_norm(h) * lng_ref[...] + lnb_ref[...]
    o_ref[...] = jnp.swapaxes(h2.reshape(TOK_TILE // BATCH, BATCH, D), 0, 1)


def _combine(dest, h1, route, g2, ln2g, ln2b, yb):
    row_spec = pl.BlockSpec((TOK_TILE, D), lambda i, *_: (i, 0))
    vec_spec = pl.BlockSpec((1, D), lambda i, *_: (0, 0))
    return pl.pallas_call(
        _combine_kernel,
        out_shape=jax.ShapeDtypeStruct((BATCH, SEQ, D), F32),
        grid_spec=pltpu.PrefetchScalarGridSpec(
            num_scalar_prefetch=1,
            grid=(T // TOK_TILE,),
            in_specs=[row_spec,
                      pl.BlockSpec((TOK_TILE, ROUTE_W), lambda i, *_: (i, 0)),
                      pl.BlockSpec((BATCH, D), lambda i, *_: (0, 0)),
                      vec_spec, vec_spec,
                      pl.BlockSpec(memory_space=pl.ANY)],
            out_specs=pl.BlockSpec((BATCH, TOK_TILE // BATCH, D), lambda i, *_: (0, i, 0)),
            scratch_shapes=[pltpu.VMEM((2, TOP_K, TOK_TILE, SUBLANES, LANES), F32),
                            pltpu.SemaphoreType.DMA((2,))],
        ),
        compiler_params=pltpu.CompilerParams(dimension_semantics=("arbitrary",),
                                             vmem_limit_bytes=32 * 1024 * 1024),
        name="moe_combine",
    )(dest, h1, route, g2, ln2g, ln2b, yb)


def _routing_plan(route, counts):
    top_idx = route[:, :TOP_K].astype(jnp.int32)
    slot = route[:, 2 * TOP_K:3 * TOP_K].astype(jnp.int32)
    counts = counts[0, :N_EXPERTS].astype(jnp.int32)
    padded = (counts + EBLK - 1) // EBLK * EBLK
    pend = jnp.cumsum(padded)
    pstart = pend - padded
    onehot = top_idx[:, :, None] == jnp.arange(N_EXPERTS, dtype=jnp.int32)
    dest = jnp.sum(jnp.where(onehot, pstart, 0), axis=-1) + slot
    n_active = (pend[-1] // EBLK).astype(jnp.int32).reshape(1)
    blk_start = jnp.arange(N_EBLK, dtype=jnp.int32) * EBLK
    block_e = jnp.minimum((pend[None, :] <= blk_start[:, None]).astype(jnp.int32).sum(1), N_EXPERTS - 1)
    zrow = jnp.maximum(pend - EBLK, 0).astype(jnp.int32)
    zflag = (padded > 0).astype(jnp.int32)
    return dest.reshape(-1).astype(jnp.int32), block_e.astype(jnp.int32), n_active, zrow, zflag


def kernel(x, c, w_ada, b_ada, w_in, b_in, w_cv_dw, b_cv_dw, ln_cv_g, ln_cv_b, w_cv_out, w_lru_conv, b_lru_conv, w_lru_a, b_lru_a, w_lru_x, b_lru_x, lru_lambda, w_lru_out, w_o, b_o, ln1_g, ln1_b, w_router, b_router, w_up, b_up, w_down, b_down, ln2_g, ln2_b):
    def vec(v):
        return v.reshape(1, -1)

    mod = _modulation(c, w_ada[0], b_ada[0])

    w_r = jnp.pad(w_router[0], ((0, 0), (0, ROUTE_W - N_EXPERTS)))
    w_r_hi = w_r.astype(BF16)
    w_r = jnp.concatenate([w_r_hi, (w_r - w_r_hi.astype(F32)).astype(BF16)], axis=1)
    b_r = jnp.pad(b_router[0], (0, ROUTE_W - N_EXPERTS), constant_values=NEG_BIG)
    h1, u2, route, counts = _mixer(
        x, mod, w_in[0].astype(BF16), vec(b_in[0]), w_cv_dw[0], vec(b_cv_dw[0]), vec(ln_cv_g[0]),
        vec(ln_cv_b[0]), w_cv_out[0].astype(BF16), w_lru_conv[0], vec(b_lru_conv[0]),
        w_lru_a[0].astype(BF16), vec(b_lru_a[0]), w_lru_x[0].astype(BF16), vec(b_lru_x[0]),
        vec(lru_lambda[0]), w_lru_out[0].astype(BF16), w_o[0].astype(BF16), vec(b_o[0]),
        vec(ln1_g[0]), vec(ln1_b[0]), w_r, vec(b_r))

    dest, block_e, n_active, zrow, zflag = _routing_plan(route, counts)
    xb = _dispatch(dest, zrow, zflag, n_active, u2)

    bg = b_up[0][:, 0::2].reshape(N_EXPERTS, 1, D_EXPERT)
    bl = b_up[0][:, 1::2].reshape(N_EXPERTS, 1, D_EXPERT)
    bd = b_down[0].reshape(N_EXPERTS, 1, D)
    yb = _experts(block_e, n_active, xb, w_up[0], bg, bl, w_down[0], bd)

    return _combine(dest, h1, route, mod[:, 5 * D:], vec(ln2_g[0]), vec(ln2_b[0]), yb)
```

```python
import functools

import jax
import jax.numpy as jnp
from jax import lax
from jax.experimental import pallas as pl
from jax.experimental.pallas import tpu as pltpu

F32 = jnp.float32
BF16 = jnp.bfloat16
HIGHEST = lax.Precision.HIGHEST

D = 1024
BATCH = 8
SEQ = 2048
T = BATCH * SEQ
CONV_K = 31
LRU_CONV_K = 4
LRU_HEADS = 8
LRU_HEAD_DIM = D // LRU_HEADS
LRU_C = 8.0
N_EXPERTS = 32
TOP_K = 4
D_EXPERT = D
SWIGLU_LIMIT = 7.0
SWIGLU_ALPHA = 1.702
LN_EPS = 1e-5
DEEPNORM_ALPHA = 2.0 ** 0.25
IN_COLS = 6 * D

LANES = 128
SUBLANES = 8
MXU_N = 256
assert D == SUBLANES * LANES

TS = 64
ROWS = TS * BATCH
CV_HIST = (CONV_K - 1) * BATCH
LRU_HIST = (LRU_CONV_K - 1) * BATCH
CONV_CHUNK = 64

EBLK = 512
N_EBLK = (T * TOP_K + N_EXPERTS * (EBLK - 1) + EBLK - 1) // EBLK
N_EROWS = N_EBLK * EBLK
DISP_TILE = 1024
TOK_TILE = 512

ROUTE_W = LANES
NEG_BIG = -1e30


def _layer_norm(x):
    mu = jnp.mean(x, axis=-1, keepdims=True)
    xc = x - mu
    var = jnp.mean(xc * xc, axis=-1, keepdims=True)
    return xc * lax.rsqrt(var + LN_EPS)


def _per_batch(v, scale=None, shift=None):
    rows = v.shape[0]
    v3 = v.reshape(rows // BATCH, BATCH, D)
    if scale is not None:
        v3 = v3 * scale[None]
    if shift is not None:
        v3 = v3 + shift[None]
    return v3.reshape(rows, D)


def _bdot(a, w):
    return jnp.dot(a.astype(BF16), w, preferred_element_type=F32)


def _to_tiles(x):
    rows = x.shape[0]
    cols = [x[:, s * LANES:(s + 1) * LANES].reshape(rows // SUBLANES, SUBLANES, LANES) for s in range(SUBLANES)]
    return jnp.swapaxes(jnp.stack(cols, axis=1), 1, 2).reshape(rows, SUBLANES, LANES)


def _from_tiles(g):
    rows = g.shape[0]
    h = jnp.swapaxes(g.reshape(rows // SUBLANES, SUBLANES, SUBLANES, LANES), 1, 2)
    return jnp.concatenate([h[:, s].reshape(rows, LANES) for s in range(SUBLANES)], axis=-1)


def _mod_kernel(c_ref, w_ref, b_ref, o_ref):
    c = c_ref[...]
    o_ref[...] = jnp.dot(c * jax.nn.sigmoid(c), w_ref[...], preferred_element_type=F32,
                         precision=HIGHEST) + b_ref[...]


def _modulation(c, w_ada, b_ada):
    tn = 1024
    return pl.pallas_call(
        _mod_kernel,
        out_shape=jax.ShapeDtypeStruct((BATCH, 6 * D), F32),
        grid=(6 * D // tn,),
        in_specs=[pl.BlockSpec((BATCH, D), lambda j: (0, 0)),
                  pl.BlockSpec((D, tn), lambda j: (0, j)),
                  pl.BlockSpec((1, tn), lambda j: (0, j))],
        out_specs=pl.BlockSpec((BATCH, tn), lambda j: (0, j)),
        name="adaln_mod",
    )(c, w_ada, b_ada.reshape(1, 6 * D))


def _mixer_kernel(x_ref, mod_ref, w_in_ref, b_in_ref, w_cvdw_ref, b_cvdw_ref, lncv_g_ref, lncv_b_ref,
                  w_cvout_ref, w_lconv_ref, b_lconv_ref, w_la_ref, b_la_ref, w_lx_ref, b_lx_ref,
                  lam_ref, w_lout_ref, w_o_ref, b_o_ref, ln1g_ref, ln1b_ref, w_r_ref, b_r_ref, tri_ref,
                  h1_ref, u2_ref, route_ref, counts_ref,
                  cvbuf, cvo_s, lbuf, a_s, u_s, hs_s, hstate, cnt_s):
    i = pl.program_id(0)

    @pl.when(i == 0)
    def _():
        cvbuf[0:CV_HIST, :] = jnp.zeros((CV_HIST, D), F32)
        lbuf[0:LRU_HIST, :] = jnp.zeros((LRU_HIST, D), F32)
        hstate[...] = jnp.zeros((BATCH, D), F32)
        cnt_s[...] = jnp.zeros((1, ROUTE_W), F32)

    def mod(j):
        return mod_ref[:, j * D:(j + 1) * D]

    x = jnp.swapaxes(x_ref[...], 0, 1).reshape(ROWS, D)
    u = _per_batch(_layer_norm(x), 1.0 + mod(1), mod(0))
    ub = u.astype(BF16)

    def in_proj(lo, hi):
        return jnp.dot(ub, w_in_ref[:, lo:hi], preferred_element_type=F32) + b_in_ref[:, lo:hi]

    zc = in_proj(0, 2 * D)
    cvbuf[CV_HIST:CV_HIST + ROWS, :] = zc[:, :D] * jax.nn.sigmoid(zc[:, D:])
    lbuf[LRU_HIST:LRU_HIST + ROWS, :] = in_proj(2 * D, 3 * D)
    zg = in_proj(3 * D, 4 * D)
    zm = in_proj(4 * D, 6 * D)

    for rc in range(ROWS // CONV_CHUNK):
        r0 = rc * CONV_CHUNK
        for c in range(D // LANES):
            ls = slice(c * LANES, (c + 1) * LANES)
            acc = jnp.broadcast_to(b_cvdw_ref[:, ls], (CONV_CHUNK, LANES))
            for k in range(CONV_K):
                acc = acc + w_cvdw_ref[k:k + 1, ls] * cvbuf[r0 + BATCH * k:r0 + BATCH * k + CONV_CHUNK, ls]
            cvo_s[r0:r0 + CONV_CHUNK, ls] = acc
    cvbuf[0:CV_HIST, :] = cvbuf[ROWS:ROWS + CV_HIST, :]

    ya = (_layer_norm(cvo_s[...]) * lncv_g_ref[...] + lncv_b_ref[...]).astype(BF16)
    ya = jnp.dot(ya * jax.nn.sigmoid(ya), w_cvout_ref[...], preferred_element_type=F32)

    xl = jnp.broadcast_to(b_lconv_ref[...], (ROWS, D))
    for k in range(LRU_CONV_K):
        xl = xl + w_lconv_ref[k:k + 1, :] * lbuf[BATCH * k:BATCH * k + ROWS, :]
    lbuf[0:LRU_HIST, :] = lbuf[ROWS:ROWS + LRU_HIST, :]

    ga, gx = [], []
    for h in range(LRU_HEADS):
        xh = xl[:, h * LRU_HEAD_DIM:(h + 1) * LRU_HEAD_DIM].astype(BF16)
        ga.append(jnp.dot(xh, w_la_ref[h], preferred_element_type=F32))
        gx.append(jnp.dot(xh, w_lx_ref[h], preferred_element_type=F32))
    gate_a = jax.nn.sigmoid(jnp.concatenate(ga, axis=-1) + b_la_ref[...])
    gate_x = jax.nn.sigmoid(jnp.concatenate(gx, axis=-1) + b_lx_ref[...])
    nlam = -lam_ref[...]
    softplus = jnp.maximum(nlam, 0.0) + jnp.log1p(jnp.exp(-jnp.abs(nlam)))
    log_a = (-LRU_C * gate_a) * softplus
    a = jnp.exp(log_a)
    mult = jnp.sqrt(1.0 - a * a)
    gated = gate_x * xl
    a_s[...] = a
    u_s[...] = mult * gated
    u_s[0:BATCH, :] = jnp.where(i == 0, gated[0:BATCH, :], u_s[0:BATCH, :])

    h = hstate[...]
    for t in range(TS):
        h = a_s[t * BATCH:(t + 1) * BATCH, :] * h + u_s[t * BATCH:(t + 1) * BATCH, :]
        hs_s[t * BATCH:(t + 1) * BATCH, :] = h
    hstate[...] = h

    gelu = 0.5 * zg * (1.0 + jnp.tanh(0.7978845608028654 * (zg + 0.044715 * (zg * zg * zg))))
    yb = _bdot(hs_s[...] * gelu, w_lout_ref[...])

    gm = jax.nn.sigmoid(zm.astype(BF16))
    merged = gm[:, :D] * ya.astype(BF16) + gm[:, D:] * yb.astype(BF16)
    mix = jnp.dot(merged, w_o_ref[...], preferred_element_type=F32) + b_o_ref[...]
    h1 = _layer_norm(DEEPNORM_ALPHA * x + _per_batch(mix, mod(2))) * ln1g_ref[...] + ln1b_ref[...]
    h1_ref[...] = h1

    u2 = _per_batch(_layer_norm(h1), 1.0 + mod(4), mod(3))
    u2_ref[...] = _to_tiles(u2)
    u2_hi = u2.astype(BF16)
    u2_lo = (u2 - u2_hi.astype(F32)).astype(BF16)
    p_hi = jnp.dot(u2_hi, w_r_ref[...], preferred_element_type=F32)
    p_lo = jnp.dot(u2_lo, w_r_ref[...], preferred_element_type=F32)
    logits = (p_hi[:, :ROUTE_W] + (p_hi[:, ROUTE_W:] + p_lo[:, :ROUTE_W] + p_lo[:, ROUTE_W:])) + b_r_ref[...]
    lane = lax.broadcasted_iota(jnp.int32, (ROWS, ROUTE_W), 1)
    vals, idxs = [], []
    for _ in range(TOP_K):
        m = jnp.max(logits, axis=-1, keepdims=True)
        idx = jnp.min(jnp.where(logits == m, lane, ROUTE_W), axis=-1, keepdims=True)
        vals.append(m)
        idxs.append(idx)
        logits = jnp.where(lane == idx, -jnp.inf, logits)
    exps = [jnp.exp(v - vals[0]) for v in vals]
    denom = exps[0] + exps[1] + exps[2] + exps[3]
    route = jnp.zeros((ROWS, ROUTE_W), F32)
    for k in range(TOP_K):
        route = jnp.where(lane == k, idxs[k].astype(F32), route)
        route = jnp.where(lane == TOP_K + k, exps[k] / denom, route)

    sel = jnp.zeros((ROWS, ROUTE_W), F32)
    for k in range(TOP_K):
        sel = jnp.where(lane == idxs[k], 1.0, sel)
    seen = jnp.dot(tri_ref[...], sel.astype(BF16), preferred_element_type=F32) + cnt_s[...]
    for k in range(TOP_K):
        slot = jnp.sum(jnp.where(lane == idxs[k], seen - 1.0, 0.0), axis=-1, keepdims=True)
        route = jnp.where(lane == 2 * TOP_K + k, slot, route)
    route_ref[...] = route
    cnt_s[...] = seen[ROWS - 1:ROWS, :]
    counts_ref[...] = jnp.broadcast_to(seen[ROWS - 1:ROWS, :], (SUBLANES, ROUTE_W))


def _const_spec(shape):
    nd = len(shape)
    return pl.BlockSpec(shape, lambda i: (0,) * nd, pipeline_mode=pl.Buffered(1))


def _mixer(xt, mod, w_in, b_in, w_cvdw, b_cvdw, lncv_g, lncv_b, w_cvout, w_lconv, b_lconv,
           w_la, b_la, w_lx, b_lx, lam, w_lout, w_o, b_o, ln1g, ln1b, w_r, b_r):
    tri = (jnp.arange(ROWS, dtype=jnp.int32)[:, None] >= jnp.arange(ROWS, dtype=jnp.int32)[None, :]).astype(BF16)
    consts = (mod, w_in, b_in, w_cvdw, b_cvdw, lncv_g, lncv_b, w_cvout, w_lconv, b_lconv,
              w_la, b_la, w_lx, b_lx, lam, w_lout, w_o, b_o, ln1g, ln1b, w_r, b_r, tri)
    row_spec = pl.BlockSpec((ROWS, D), lambda i: (i, 0))
    return pl.pallas_call(
        _mixer_kernel,
        out_shape=(jax.ShapeDtypeStruct((T, D), F32), jax.ShapeDtypeStruct((T, SUBLANES, LANES), F32),
                   jax.ShapeDtypeStruct((T, ROUTE_W), F32),
                   jax.ShapeDtypeStruct((SUBLANES, ROUTE_W), F32)),
        grid=(SEQ // TS,),
        in_specs=[pl.BlockSpec((BATCH, TS, D), lambda i: (0, i, 0))] + [_const_spec(a.shape) for a in consts],
        out_specs=(row_spec, pl.BlockSpec((ROWS, SUBLANES, LANES), lambda i: (i, 0, 0)),
                   pl.BlockSpec((ROWS, ROUTE_W), lambda i: (i, 0)),
                   pl.BlockSpec((SUBLANES, ROUTE_W), lambda i: (0, 0))),
        scratch_shapes=[pltpu.VMEM((CV_HIST + ROWS, D), F32),
                        pltpu.VMEM((ROWS, D), F32),
                        pltpu.VMEM((LRU_HIST + ROWS, D), F32),
                        pltpu.VMEM((ROWS, D), F32),
                        pltpu.VMEM((ROWS, D), F32),
                        pltpu.VMEM((ROWS, D), F32),
                        pltpu.VMEM((BATCH, D), F32),
                        pltpu.VMEM((1, ROUTE_W), F32)],
        compiler_params=pltpu.CompilerParams(dimension_semantics=("arbitrary",),
                                             vmem_limit_bytes=58 * 1024 * 1024),
        name="mixer_router",
    )(xt, *consts)


def _dispatch_kernel(dest_ref, zrow_ref, zflag_ref, na_ref, u_ref, xb_hbm, zbuf, zsem, sem):
    i = pl.program_id(0)

    def zero_block(row):
        return pltpu.make_async_copy(zbuf, xb_hbm.at[pl.ds(pl.multiple_of(row, EBLK), EBLK)], zsem)

    @pl.when(i == 0)
    def _():
        zbuf[...] = jnp.zeros((EBLK, SUBLANES, LANES), F32)
        for e in range(N_EXPERTS):
            @pl.when(zflag_ref[e] > 0)
            def _():
                zero_block(zrow_ref[e]).start()

        def tail_start(b, carry):
            zero_block(b * EBLK).start()
            return carry

        def tail_wait(b, carry):
            zero_block(b * EBLK).wait()
            return carry

        lax.fori_loop(na_ref[0], N_EBLK, tail_start, 0)
        for e in range(N_EXPERTS):
            @pl.when(zflag_ref[e] > 0)
            def _():
                zero_block(zrow_ref[e]).wait()
        lax.fori_loop(na_ref[0], N_EBLK, tail_wait, 0)

    def issue(r, carry):
        for k in range(TOP_K):
            d = dest_ref[(i * DISP_TILE + r) * TOP_K + k]
            pltpu.make_async_copy(u_ref.at[r], xb_hbm.at[d], sem).start(priority=k % 2)
        return carry

    lax.fori_loop(0, DISP_TILE, issue, 0, unroll=8)
    for k in range(TOP_K):
        pltpu.make_async_copy(u_ref, xb_hbm.at[pl.ds(0, DISP_TILE)], sem).wait()


def _dispatch(dest, zrow, zflag, n_active, u2):
    return pl.pallas_call(
        _dispatch_kernel,
        out_shape=jax.ShapeDtypeStruct((N_EROWS, SUBLANES, LANES), F32),
        grid_spec=pltpu.PrefetchScalarGridSpec(
            num_scalar_prefetch=4,
            grid=(T // DISP_TILE,),
            in_specs=[pl.BlockSpec((DISP_TILE, SUBLANES, LANES), lambda i, *_: (i, 0, 0))],
            out_specs=pl.BlockSpec(memory_space=pl.ANY),
            scratch_shapes=[pltpu.VMEM((EBLK, SUBLANES, LANES), F32), pltpu.SemaphoreType.DMA,
                            pltpu.SemaphoreType.DMA],
        ),
        compiler_params=pltpu.CompilerParams(dimension_semantics=("arbitrary",),
                                             vmem_limit_bytes=24 * 1024 * 1024),
        name="moe_dispatch",
    )(dest, zrow, zflag, n_active, u2)


def _expert_kernel(be_ref, na_ref, nv_ref, x_ref, wup_ref, bg_ref, bl_ref, wdn_ref, bd_ref, perm_ref, o_ref,
                   wg_s, wl_s, wd_s):
    b = pl.program_id(0)
    active = b < na_ref[0]
    new_expert = jnp.logical_or(b == 0, be_ref[b] != be_ref[jnp.maximum(b - 1, 0)])

    @pl.when(jnp.logical_and(active, new_expert))
    def _():
        for s in range(2 * D_EXPERT // MXU_N):
            slab = wup_ref[0, :, s * MXU_N:(s + 1) * MXU_N].astype(BF16)
            split = jnp.dot(slab, perm_ref[...], preferred_element_type=F32).astype(BF16)
            half = MXU_N // 2
            wg_s[:, s * half:(s + 1) * half] = split[:, :half]
            wl_s[:, s * half:(s + 1) * half] = split[:, half:]
        wd_s[...] = wdn_ref[0].astype(BF16)

    def ffn(rows):
        xb = _from_tiles(x_ref[0:rows]).astype(BF16)
        zg = jnp.dot(xb, wg_s[...], preferred_element_type=F32) + bg_ref[0]
        zl = jnp.dot(xb, wl_s[...], preferred_element_type=F32) + bl_ref[0]
        zg = jnp.minimum(zg, SWIGLU_LIMIT)
        zl = jnp.clip(zl, -SWIGLU_LIMIT, SWIGLU_LIMIT)
        act = zg * jax.nn.sigmoid(SWIGLU_ALPHA * zg) * (zl + 1.0)
        y = jnp.dot(act.astype(BF16), wd_s[...], preferred_element_type=F32) + bd_ref[0]
        o_ref[0:rows] = _to_tiles(y)

    half_only = nv_ref[b] <= EBLK // 2

    @pl.when(jnp.logical_and(active, jnp.logical_not(half_only)))
    def _():
        ffn(EBLK)

    @pl.when(jnp.logical_and(active, half_only))
    def _():
        ffn(EBLK // 2)
        o_ref[EBLK // 2:EBLK] = jnp.zeros((EBLK // 2, SUBLANES, LANES), F32)

    @pl.when(jnp.logical_not(active))
    def _():
        o_ref[...] = jnp.zeros((EBLK, SUBLANES, LANES), F32)


def _experts(block_e, n_active, block_valid, xb, w_up, bg, bl, w_down, bd):
    def row_map(b, be, na, nv):
        return (jnp.maximum(jnp.minimum(b, na[0] - 1), 0), 0, 0)

    def w_map(b, be, na, nv):
        return (be[b], 0, 0)

    src = jnp.arange(MXU_N, dtype=jnp.int32)[:, None]
    dst = jnp.arange(MXU_N, dtype=jnp.int32)[None, :]
    half = MXU_N // 2
    perm = jnp.where(dst < half, src == 2 * dst, src == 2 * (dst - half) + 1).astype(BF16)

    return pl.pallas_call(
        _expert_kernel,
        out_shape=jax.ShapeDtypeStruct((N_EROWS, SUBLANES, LANES), F32),
        grid_spec=pltpu.PrefetchScalarGridSpec(
            num_scalar_prefetch=3,
            grid=(N_EBLK,),
            in_specs=[pl.BlockSpec((EBLK, SUBLANES, LANES), row_map),
                      pl.BlockSpec((1, D, 2 * D_EXPERT), w_map),
                      pl.BlockSpec((1, 1, D_EXPERT), w_map),
                      pl.BlockSpec((1, 1, D_EXPERT), w_map),
                      pl.BlockSpec((1, D_EXPERT, D), w_map),
                      pl.BlockSpec((1, 1, D), w_map),
                      pl.BlockSpec((MXU_N, MXU_N), lambda b, be, na, nv: (0, 0))],
            out_specs=pl.BlockSpec((EBLK, SUBLANES, LANES), lambda b, be, na, nv: (b, 0, 0)),
            scratch_shapes=[pltpu.VMEM((D, D_EXPERT), BF16), pltpu.VMEM((D, D_EXPERT), BF16),
                            pltpu.VMEM((D_EXPERT, D), BF16)],
        ),
        compiler_params=pltpu.CompilerParams(dimension_semantics=("arbitrary",),
                                             vmem_limit_bytes=52 * 1024 * 1024),
        name="moe_experts",
    )(block_e, n_active, block_valid, xb, w_up, bg, bl, w_down, bd, perm)


def _combine_kernel(dest_ref, h1_ref, route_ref, g2_ref, lng_ref, lnb_ref, yb_hbm, o_ref, gbuf, sem):
    i = pl.program_id(0)
    slot = i % 2

    def issue(tile, buf):
        def body(r, carry):
            for k in range(TOP_K):
                d = dest_ref[(tile * TOK_TILE + r) * TOP_K + k]
                pltpu.make_async_copy(yb_hbm.at[d], gbuf.at[buf, k, r], sem.at[buf]).start(priority=k % 2)
            return carry

        lax.fori_loop(0, TOK_TILE, body, 0, unroll=8)

    @pl.when(i == 0)
    def _():
        issue(0, 0)

    for k in range(TOP_K):
        pltpu.make_async_copy(yb_hbm.at[pl.ds(0, TOK_TILE)], gbuf.at[slot, k], sem.at[slot]).wait()

    @pl.when(i + 1 < pl.num_programs(0))
    def _():
        issue(i + 1, 1 - slot)

    route = route_ref[...]
    y = route[:, TOP_K:TOP_K + 1] * _from_tiles(gbuf[slot, 0])
    for k in range(1, TOP_K):
        y = y + route[:, TOP_K + k:TOP_K + k + 1] * _from_tiles(gbuf[slot, k])
    h = DEEPNORM_ALPHA * h1_ref[...] + _per_batch(y, g2_ref[...])
    h2 = _layer_norm(h) * lng_ref[...] + lnb_ref[...]
    o_ref[...] = jnp.swapaxes(h2.reshape(TOK_TILE // BATCH, BATCH, D), 0, 1)


def _combine(dest, h1, route, g2, ln2g, ln2b, yb):
    row_spec = pl.BlockSpec((TOK_TILE, D), lambda i, *_: (i, 0))
    vec_spec = pl.BlockSpec((1, D), lambda i, *_: (0, 0))
    return pl.pallas_call(
        _combine_kernel,
        out_shape=jax.ShapeDtypeStruct((BATCH, SEQ, D), F32),
        grid_spec=pltpu.PrefetchScalarGridSpec(
            num_scalar_prefetch=1,
            grid=(T // TOK_TILE,),
            in_specs=[row_spec,
                      pl.BlockSpec((TOK_TILE, ROUTE_W), lambda i, *_: (i, 0)),
                      pl.BlockSpec((BATCH, D), lambda i, *_: (0, 0)),
                      vec_spec, vec_spec,
                      pl.BlockSpec(memory_space=pl.ANY)],
            out_specs=pl.BlockSpec((BATCH, TOK_TILE // BATCH, D), lambda i, *_: (0, i, 0)),
            scratch_shapes=[pltpu.VMEM((2, TOP_K, TOK_TILE, SUBLANES, LANES), F32),
                            pltpu.SemaphoreType.DMA((2,))],
        ),
        compiler_params=pltpu.CompilerParams(dimension_semantics=("arbitrary",),
                                             vmem_limit_bytes=48 * 1024 * 1024),
        name="moe_combine",
    )(dest, h1, route, g2, ln2g, ln2b, yb)


def _routing_plan(route, counts):
    top_idx = route[:, :TOP_K].astype(jnp.int32)
    slot = route[:, 2 * TOP_K:3 * TOP_K].astype(jnp.int32)
    counts = counts[0, :N_EXPERTS].astype(jnp.int32)
    padded = (counts + EBLK - 1) // EBLK * EBLK
    pend = jnp.cumsum(padded)
    pstart = pend - padded
    onehot = top_idx[:, :, None] == jnp.arange(N_EXPERTS, dtype=jnp.int32)
    dest = jnp.sum(jnp.where(onehot, pstart, 0), axis=-1) + slot
    n_active = (pend[-1] // EBLK).astype(jnp.int32).reshape(1)
    blk_start = jnp.arange(N_EBLK, dtype=jnp.int32) * EBLK
    block_e = jnp.minimum((pend[None, :] <= blk_start[:, None]).astype(jnp.int32).sum(1), N_EXPERTS - 1)
    seg_end = (pstart + counts)[block_e]
    block_valid = jnp.clip(seg_end - blk_start, 0, EBLK).astype(jnp.int32)
    zrow = jnp.maximum(pend - EBLK, 0).astype(jnp.int32)
    zflag = (padded > 0).astype(jnp.int32)
    return dest.reshape(-1).astype(jnp.int32), block_e.astype(jnp.int32), n_active, block_valid, zrow, zflag


def kernel(x, c, w_ada, b_ada, w_in, b_in, w_cv_dw, b_cv_dw, ln_cv_g, ln_cv_b, w_cv_out, w_lru_conv, b_lru_conv, w_lru_a, b_lru_a, w_lru_x, b_lru_x, lru_lambda, w_lru_out, w_o, b_o, ln1_g, ln1_b, w_router, b_router, w_up, b_up, w_down, b_down, ln2_g, ln2_b):
    def vec(v):
        return v.reshape(1, -1)

    mod = _modulation(c, w_ada[0], b_ada[0])

    w_r = jnp.pad(w_router[0], ((0, 0), (0, ROUTE_W - N_EXPERTS)))
    w_r_hi = w_r.astype(BF16)
    w_r = jnp.concatenate([w_r_hi, (w_r - w_r_hi.astype(F32)).astype(BF16)], axis=1)
    b_r = jnp.pad(b_router[0], (0, ROUTE_W - N_EXPERTS), constant_values=NEG_BIG)
    h1, u2, route, counts = _mixer(
        x, mod, w_in[0].astype(BF16), vec(b_in[0]), w_cv_dw[0], vec(b_cv_dw[0]), vec(ln_cv_g[0]),
        vec(ln_cv_b[0]), w_cv_out[0].astype(BF16), w_lru_conv[0], vec(b_lru_conv[0]),
        w_lru_a[0].astype(BF16), vec(b_lru_a[0]), w_lru_x[0].astype(BF16), vec(b_lru_x[0]),
        vec(lru_lambda[0]), w_lru_out[0].astype(BF16), w_o[0].astype(BF16), vec(b_o[0]),
        vec(ln1_g[0]), vec(ln1_b[0]), w_r, vec(b_r))

    dest, block_e, n_active, block_valid, zrow, zflag = _routing_plan(route, counts)
    xb = _dispatch(dest, zrow, zflag, n_active, u2)

    bg = b_up[0][:, 0::2].reshape(N_EXPERTS, 1, D_EXPERT)
    bl = b_up[0][:, 1::2].reshape(N_EXPERTS, 1, D_EXPERT)
    bd = b_down[0].reshape(N_EXPERTS, 1, D)
    yb = _experts(block_e, n_active, block_valid, xb, w_up[0], bg, bl, w_down[0], bd)

    return _combine(dest, h1, route, mod[:, 5 * D:], vec(ln2_g[0]), vec(ln2_b[0]), yb)
```

```python
import functools

import jax
import jax.numpy as jnp
from jax import lax
from jax.experimental import pallas as pl
from jax.experimental.pallas import tpu as pltpu

F32 = jnp.float32
BF16 = jnp.bfloat16
HIGHEST = lax.Precision.HIGHEST

D = 1024
BATCH = 8
SEQ = 2048
T = BATCH * SEQ
CONV_K = 31
LRU_CONV_K = 4
LRU_HEADS = 8
LRU_HEAD_DIM = D // LRU_HEADS
LRU_C = 8.0
N_EXPERTS = 32
TOP_K = 4
D_EXPERT = D
SWIGLU_LIMIT = 7.0
SWIGLU_ALPHA = 1.702
LN_EPS = 1e-5
DEEPNORM_ALPHA = 2.0 ** 0.25
IN_COLS = 6 * D

LANES = 128
SUBLANES = 8
MXU_N = 256
assert D == SUBLANES * LANES

TS = 64
ROWS = TS * BATCH
CV_HIST = (CONV_K - 1) * BATCH
LRU_HIST = (LRU_CONV_K - 1) * BATCH
CONV_CHUNK = 64

EBLK = 512
N_EBLK = (T * TOP_K + N_EXPERTS * (EBLK - 1) + EBLK - 1) // EBLK
N_EROWS = N_EBLK * EBLK
DISP_TILE = 1024
TOK_TILE = 256

ROUTE_W = LANES
NEG_BIG = -1e30


def _layer_norm(x):
    mu = jnp.mean(x, axis=-1, keepdims=True)
    xc = x - mu
    var = jnp.mean(xc * xc, axis=-1, keepdims=True)
    return xc * lax.rsqrt(var + LN_EPS)


def _per_batch(v, scale=None, shift=None):
    rows = v.shape[0]
    v3 = v.reshape(rows // BATCH, BATCH, D)
    if scale is not None:
        v3 = v3 * scale[None]
    if shift is not None:
        v3 = v3 + shift[None]
    return v3.reshape(rows, D)


def _bdot(a, w):
    return jnp.dot(a.astype(BF16), w, preferred_element_type=F32)


def _to_tiles(x):
    rows = x.shape[0]
    cols = [x[:, s * LANES:(s + 1) * LANES].reshape(rows // SUBLANES, SUBLANES, LANES) for s in range(SUBLANES)]
    return jnp.swapaxes(jnp.stack(cols, axis=1), 1, 2).reshape(rows, SUBLANES, LANES)


def _from_tiles(g):
    rows = g.shape[0]
    h = jnp.swapaxes(g.reshape(rows // SUBLANES, SUBLANES, SUBLANES, LANES), 1, 2)
    return jnp.concatenate([h[:, s].reshape(rows, LANES) for s in range(SUBLANES)], axis=-1)


def _zero_after(v):
    return jnp.where(v[0:SUBLANES, 0:LANES] > 0.0, 1.0, 0.0) * 0.0


def _mod_kernel(c_ref, w_ref, b_ref, o_ref):
    c = c_ref[...]
    o_ref[...] = jnp.dot(c * jax.nn.sigmoid(c), w_ref[...], preferred_element_type=F32,
                         precision=HIGHEST) + b_ref[...]


def _modulation(c, w_ada, b_ada):
    tn = 1024
    return pl.pallas_call(
        _mod_kernel,
        out_shape=jax.ShapeDtypeStruct((BATCH, 6 * D), F32),
        grid=(6 * D // tn,),
        in_specs=[pl.BlockSpec((BATCH, D), lambda j: (0, 0)),
                  pl.BlockSpec((D, tn), lambda j: (0, j)),
                  pl.BlockSpec((1, tn), lambda j: (0, j))],
        out_specs=pl.BlockSpec((BATCH, tn), lambda j: (0, j)),
        name="adaln_mod",
    )(c, w_ada, b_ada.reshape(1, 6 * D))


def _mixer_kernel(x_ref, mod_ref, w_in_ref, b_in_ref, w_cvdw_ref, b_cvdw_ref, lncv_g_ref, lncv_b_ref,
                  w_cvout_ref, w_lconv_ref, b_lconv_ref, w_la_ref, b_la_ref, w_lx_ref, b_lx_ref,
                  lam_ref, w_lout_ref, w_o_ref, b_o_ref, ln1g_ref, ln1b_ref, w_r_ref, b_r_ref,
                  h1_ref, u2_ref, route_ref, counts_ref,
                  cvbuf, cvo_s, lbuf, a_s, u_s, hs_s, hstate, cnt_s):
    i = pl.program_id(0)

    @pl.when(i == 0)
    def _():
        cvbuf[0:CV_HIST, :] = jnp.zeros((CV_HIST, D), F32)
        lbuf[0:LRU_HIST, :] = jnp.zeros((LRU_HIST, D), F32)
        hstate[...] = jnp.zeros((BATCH, D), F32)
        cnt_s[...] = jnp.zeros((1, ROUTE_W), F32)

    def mod(j):
        return mod_ref[:, j * D:(j + 1) * D]

    x = jnp.swapaxes(x_ref[...], 0, 1).reshape(ROWS, D)
    u = _per_batch(_layer_norm(x), 1.0 + mod(1), mod(0))
    ub = u.astype(BF16)

    def in_proj(lo, hi):
        return jnp.dot(ub, w_in_ref[:, lo:hi], preferred_element_type=F32) + b_in_ref[:, lo:hi]

    n_pairs = D // MXU_N
    n_chunks = ROWS // CONV_CHUNK
    group = (IN_COLS - 2 * D) // (2 * n_pairs)
    parts = []
    for j in range(n_pairs):
        cols = slice(j * MXU_N, (j + 1) * MXU_N)
        z_val = in_proj(j * MXU_N, (j + 1) * MXU_N)
        z_gate = in_proj(D + j * MXU_N, D + (j + 1) * MXU_N)
        cvbuf[CV_HIST:CV_HIST + ROWS, cols] = z_val * jax.nn.sigmoid(z_gate)
        ties = {}
        for g in range(2):
            lo = 2 * D + (2 * j + g) * group
            parts.append(in_proj(lo, lo + group))
            ties[g * (n_chunks // 2)] = jnp.tile(_zero_after(parts[-1]), (CONV_CHUNK // SUBLANES, 1))
        for rc in range(n_chunks):
            r0 = rc * CONV_CHUNK
            for c in range(2 * j, 2 * j + 2):
                ls = slice(c * LANES, (c + 1) * LANES)
                acc = jnp.broadcast_to(b_cvdw_ref[:, ls], (CONV_CHUNK, LANES))
                if c == 2 * j and rc in ties:
                    acc = acc + ties[rc]
                for k in range(CONV_K):
                    acc = acc + w_cvdw_ref[k:k + 1, ls] * cvbuf[r0 + BATCH * k:r0 + BATCH * k + CONV_CHUNK, ls]
                cvo_s[r0:r0 + CONV_CHUNK, ls] = acc
    z_rest = jnp.concatenate(parts, axis=-1)
    lbuf[LRU_HIST:LRU_HIST + ROWS, :] = z_rest[:, :D]
    zg = z_rest[:, D:2 * D]
    zm = z_rest[:, 2 * D:]
    cvbuf[0:CV_HIST, :] = cvbuf[ROWS:ROWS + CV_HIST, :]

    ya = _layer_norm(cvo_s[...]) * lncv_g_ref[...] + lncv_b_ref[...]
    ya = ya * jax.nn.sigmoid(ya)
    ya = _bdot(ya, w_cvout_ref[...])

    xl = jnp.broadcast_to(b_lconv_ref[...], (ROWS, D))
    for k in range(LRU_CONV_K):
        xl = xl + w_lconv_ref[k:k + 1, :] * lbuf[BATCH * k:BATCH * k + ROWS, :]
    lbuf[0:LRU_HIST, :] = lbuf[ROWS:ROWS + LRU_HIST, :]

    ga, gx = [], []
    for h in range(LRU_HEADS):
        xh = xl[:, h * LRU_HEAD_DIM:(h + 1) * LRU_HEAD_DIM].astype(BF16)
        ga.append(jnp.dot(xh, w_la_ref[h], preferred_element_type=F32))
        gx.append(jnp.dot(xh, w_lx_ref[h], preferred_element_type=F32))
    gate_a = jax.nn.sigmoid(jnp.concatenate(ga, axis=-1) + b_la_ref[...])
    gate_x = jax.nn.sigmoid(jnp.concatenate(gx, axis=-1) + b_lx_ref[...])
    nlam = -lam_ref[...]
    softplus = jnp.maximum(nlam, 0.0) + jnp.log1p(jnp.exp(-jnp.abs(nlam)))
    log_a = (-LRU_C * gate_a) * softplus
    a = jnp.exp(log_a)
    mult = jnp.sqrt(1.0 - a * a)
    row = lax.broadcasted_iota(jnp.int32, (ROWS, D), 0) + i * ROWS
    mult = jnp.where(row < BATCH, 1.0, mult)
    a_s[...] = a
    u_s[...] = mult * (gate_x * xl)

    tie_ya = jnp.concatenate([_zero_after(ya), jnp.zeros((BATCH, D - LANES), F32)], axis=1)
    h = hstate[...] + tie_ya
    for t in range(TS):
        h = a_s[t * BATCH:(t + 1) * BATCH, :] * h + u_s[t * BATCH:(t + 1) * BATCH, :]
        hs_s[t * BATCH:(t + 1) * BATCH, :] = h
    hstate[...] = h

    gelu = 0.5 * zg * (1.0 + jnp.tanh(0.7978845608028654 * (zg + 0.044715 * (zg * zg * zg))))
    yb = _bdot(hs_s[...] * gelu, w_lout_ref[...])

    gm = jax.nn.sigmoid(zm)
    mix = _bdot(gm[:, :D] * ya + gm[:, D:] * yb, w_o_ref[...]) + b_o_ref[...]
    h1 = _layer_norm(DEEPNORM_ALPHA * x + _per_batch(mix, mod(2))) * ln1g_ref[...] + ln1b_ref[...]
    h1_ref[...] = h1

    u2 = _per_batch(_layer_norm(h1), 1.0 + mod(4), mod(3))
    u2_ref[...] = _to_tiles(u2)
    u2_hi = u2.astype(BF16)
    u2_lo = (u2 - u2_hi.astype(F32)).astype(BF16)
    p_hi = jnp.dot(u2_hi, w_r_ref[...], preferred_element_type=F32)
    p_lo = jnp.dot(u2_lo, w_r_ref[...], preferred_element_type=F32)
    logits = (p_hi[:, :ROUTE_W] + (p_hi[:, ROUTE_W:] + p_lo[:, :ROUTE_W] + p_lo[:, ROUTE_W:])) + b_r_ref[...]
    lane = lax.broadcasted_iota(jnp.int32, (ROWS, ROUTE_W), 1)
    vals, idxs = [], []
    for _ in range(TOP_K):
        m = jnp.max(logits, axis=-1, keepdims=True)
        idx = jnp.min(jnp.where(logits == m, lane, ROUTE_W), axis=-1, keepdims=True)
        vals.append(m)
        idxs.append(idx)
        logits = jnp.where(lane == idx, -jnp.inf, logits)
    exps = [jnp.exp(v - vals[0]) for v in vals]
    denom = exps[0] + exps[1] + exps[2] + exps[3]
    route = jnp.zeros((ROWS, ROUTE_W), F32)
    for k in range(TOP_K):
        route = jnp.where(lane == k, idxs[k].astype(F32), route)
        route = jnp.where(lane == TOP_K + k, exps[k] / denom, route)

    sel = jnp.zeros((ROWS, ROUTE_W), F32)
    for k in range(TOP_K):
        sel = jnp.where(lane == idxs[k], 1.0, sel)
    tri = (lax.broadcasted_iota(jnp.int32, (ROWS, ROWS), 0)
           >= lax.broadcasted_iota(jnp.int32, (ROWS, ROWS), 1)).astype(BF16)
    seen = jnp.dot(tri, sel.astype(BF16), preferred_element_type=F32) + cnt_s[...]
    for k in range(TOP_K):
        slot = jnp.sum(jnp.where(lane == idxs[k], seen - 1.0, 0.0), axis=-1, keepdims=True)
        route = jnp.where(lane == 2 * TOP_K + k, slot, route)
    route_ref[...] = route
    cnt_s[...] = seen[ROWS - 1:ROWS, :]
    counts_ref[...] = jnp.broadcast_to(seen[ROWS - 1:ROWS, :], (SUBLANES, ROUTE_W))


def _const_spec(shape):
    nd = len(shape)
    return pl.BlockSpec(shape, lambda i: (0,) * nd, pipeline_mode=pl.Buffered(1))


def _mixer(xt, mod, w_in, b_in, w_cvdw, b_cvdw, lncv_g, lncv_b, w_cvout, w_lconv, b_lconv,
           w_la, b_la, w_lx, b_lx, lam, w_lout, w_o, b_o, ln1g, ln1b, w_r, b_r):
    consts = (mod, w_in, b_in, w_cvdw, b_cvdw, lncv_g, lncv_b, w_cvout, w_lconv, b_lconv,
              w_la, b_la, w_lx, b_lx, lam, w_lout, w_o, b_o, ln1g, ln1b, w_r, b_r)
    row_spec = pl.BlockSpec((ROWS, D), lambda i: (i, 0))
    return pl.pallas_call(
        _mixer_kernel,
        out_shape=(jax.ShapeDtypeStruct((T, D), F32), jax.ShapeDtypeStruct((T, SUBLANES, LANES), F32),
                   jax.ShapeDtypeStruct((T, ROUTE_W), F32),
                   jax.ShapeDtypeStruct((SUBLANES, ROUTE_W), F32)),
        grid=(SEQ // TS,),
        in_specs=[pl.BlockSpec((BATCH, TS, D), lambda i: (0, i, 0))] + [_const_spec(a.shape) for a in consts],
        out_specs=(row_spec, pl.BlockSpec((ROWS, SUBLANES, LANES), lambda i: (i, 0, 0)),
                   pl.BlockSpec((ROWS, ROUTE_W), lambda i: (i, 0)),
                   pl.BlockSpec((SUBLANES, ROUTE_W), lambda i: (0, 0))),
        scratch_shapes=[pltpu.VMEM((CV_HIST + ROWS, D), F32),
                        pltpu.VMEM((ROWS, D), F32),
                        pltpu.VMEM((LRU_HIST + ROWS, D), F32),
                        pltpu.VMEM((ROWS, D), F32),
                        pltpu.VMEM((ROWS, D), F32),
                        pltpu.VMEM((ROWS, D), F32),
                        pltpu.VMEM((BATCH, D), F32),
                        pltpu.VMEM((1, ROUTE_W), F32)],
        compiler_params=pltpu.CompilerParams(dimension_semantics=("arbitrary",),
                                             vmem_limit_bytes=58 * 1024 * 1024),
        name="mixer_router",
    )(xt, *consts)


def _dispatch_kernel(dest_ref, zrow_ref, zflag_ref, na_ref, u_ref, xb_hbm, zbuf, zsem, sem):
    i = pl.program_id(0)

    def zero_block(row):
        return pltpu.make_async_copy(zbuf, xb_hbm.at[pl.ds(pl.multiple_of(row, EBLK), EBLK)], zsem)

    @pl.when(i == 0)
    def _():
        zbuf[...] = jnp.zeros((EBLK, SUBLANES, LANES), F32)
        for e in range(N_EXPERTS):
            @pl.when(zflag_ref[e] > 0)
            def _():
                zero_block(zrow_ref[e]).start()

        def tail_start(b, carry):
            zero_block(b * EBLK).start()
            return carry

        def tail_wait(b, carry):
            zero_block(b * EBLK).wait()
            return carry

        lax.fori_loop(na_ref[0], N_EBLK, tail_start, 0)
        for e in range(N_EXPERTS):
            @pl.when(zflag_ref[e] > 0)
            def _():
                zero_block(zrow_ref[e]).wait()
        lax.fori_loop(na_ref[0], N_EBLK, tail_wait, 0)

    def issue(r, carry):
        for k in range(TOP_K):
            d = dest_ref[(i * DISP_TILE + r) * TOP_K + k]
            pltpu.make_async_copy(u_ref.at[r], xb_hbm.at[d], sem).start(priority=k % 2)
        return carry

    lax.fori_loop(0, DISP_TILE, issue, 0, unroll=8)
    for k in range(TOP_K):
        pltpu.make_async_copy(u_ref, xb_hbm.at[pl.ds(0, DISP_TILE)], sem).wait()


def _dispatch(dest, zrow, zflag, n_active, u2):
    return pl.pallas_call(
        _dispatch_kernel,
        out_shape=jax.ShapeDtypeStruct((N_EROWS, SUBLANES, LANES), F32),
        grid_spec=pltpu.PrefetchScalarGridSpec(
            num_scalar_prefetch=4,
            grid=(T // DISP_TILE,),
            in_specs=[pl.BlockSpec((DISP_TILE, SUBLANES, LANES), lambda i, *_: (i, 0, 0))],
            out_specs=pl.BlockSpec(memory_space=pl.ANY),
            scratch_shapes=[pltpu.VMEM((EBLK, SUBLANES, LANES), F32), pltpu.SemaphoreType.DMA,
                            pltpu.SemaphoreType.DMA],
        ),
        compiler_params=pltpu.CompilerParams(dimension_semantics=("arbitrary",),
                                             vmem_limit_bytes=24 * 1024 * 1024),
        name="moe_dispatch",
    )(dest, zrow, zflag, n_active, u2)


def _expert_kernel(be_ref, na_ref, x_ref, wup_ref, bg_ref, bl_ref, wdn_ref, bd_ref, perm_ref, o_ref,
                   wg_s, wl_s, wd_s):
    b = pl.program_id(0)
    active = b < na_ref[0]
    new_expert = jnp.logical_or(b == 0, be_ref[b] != be_ref[jnp.maximum(b - 1, 0)])

    @pl.when(jnp.logical_and(active, new_expert))
    def _():
        for s in range(2 * D_EXPERT // MXU_N):
            slab = wup_ref[0, :, s * MXU_N:(s + 1) * MXU_N].astype(BF16)
            split = jnp.dot(slab, perm_ref[...], preferred_element_type=F32).astype(BF16)
            half = MXU_N // 2
            wg_s[:, s * half:(s + 1) * half] = split[:, :half]
            wl_s[:, s * half:(s + 1) * half] = split[:, half:]
        wd_s[...] = wdn_ref[0].astype(BF16)

    @pl.when(active)
    def _():
        xb = _from_tiles(x_ref[...]).astype(BF16)
        zg = jnp.dot(xb, wg_s[...], preferred_element_type=F32) + bg_ref[0]
        zl = jnp.dot(xb, wl_s[...], preferred_element_type=F32) + bl_ref[0]
        zg = jnp.minimum(zg, SWIGLU_LIMIT)
        zl = jnp.clip(zl, -SWIGLU_LIMIT, SWIGLU_LIMIT)
        act = zg * jax.nn.sigmoid(SWIGLU_ALPHA * zg) * (zl + 1.0)
        y = jnp.dot(act.astype(BF16), wd_s[...], preferred_element_type=F32) + bd_ref[0]
        o_ref[...] = _to_tiles(y)

    @pl.when(jnp.logical_not(active))
    def _():
        o_ref[...] = jnp.zeros((EBLK, SUBLANES, LANES), F32)


def _experts(block_e, n_active, xb, w_up, bg, bl, w_down, bd):
    def row_map(b, be, na):
        return (jnp.maximum(jnp.minimum(b, na[0] - 1), 0), 0, 0)

    def w_map(b, be, na):
        return (be[b], 0, 0)

    src = jnp.arange(MXU_N, dtype=jnp.int32)[:, None]
    dst = jnp.arange(MXU_N, dtype=jnp.int32)[None, :]
    half = MXU_N // 2
    perm = jnp.where(dst < half, src == 2 * dst, src == 2 * (dst - half) + 1).astype(BF16)

    return pl.pallas_call(
        _expert_kernel,
        out_shape=jax.ShapeDtypeStruct((N_EROWS, SUBLANES, LANES), F32),
        grid_spec=pltpu.PrefetchScalarGridSpec(
            num_scalar_prefetch=2,
            grid=(N_EBLK,),
            in_specs=[pl.BlockSpec((EBLK, SUBLANES, LANES), row_map),
                      pl.BlockSpec((1, D, 2 * D_EXPERT), w_map),
                      pl.BlockSpec((1, 1, D_EXPERT), w_map),
                      pl.BlockSpec((1, 1, D_EXPERT), w_map),
                      pl.BlockSpec((1, D_EXPERT, D), w_map),
                      pl.BlockSpec((1, 1, D), w_map),
                      pl.BlockSpec((MXU_N, MXU_N), lambda b, be, na: (0, 0))],
            out_specs=pl.BlockSpec((EBLK, SUBLANES, LANES), lambda b, be, na: (b, 0, 0)),
            scratch_shapes=[pltpu.VMEM((D, D_EXPERT), BF16), pltpu.VMEM((D, D_EXPERT), BF16),
                            pltpu.VMEM((D_EXPERT, D), BF16)],
        ),
        compiler_params=pltpu.CompilerParams(dimension_semantics=("arbitrary",),
                                             vmem_limit_bytes=52 * 1024 * 1024),
        name="moe_experts",
    )(block_e, n_active, xb, w_up, bg, bl, w_down, bd, perm)


def _combine_kernel(dest_ref, h1_ref, route_ref, g2_ref, lng_ref, lnb_ref, yb_hbm, o_ref, gbuf, sem):
    i = pl.program_id(0)
    slot = i % 2

    def issue(tile, buf):
        def body(r, carry):
            for k in range(TOP_K):
                d = dest_ref[(tile * TOK_TILE + r) * TOP_K + k]
                pltpu.make_async_copy(yb_hbm.at[d], gbuf.at[buf, k, r], sem.at[buf]).start(priority=k % 2)
            return carry

        lax.fori_loop(0, TOK_TILE, body, 0, unroll=8)

    @pl.when(i == 0)
    def _():
        issue(0, 0)

    for k in range(TOP_K):
        pltpu.make_async_copy(yb_hbm.at[pl.ds(0, TOK_TILE)], gbuf.at[slot, k], sem.at[slot]).wait()

    @pl.when(i + 1 < pl.num_programs(0))
    def _():
        issue(i + 1, 1 - slot)

    route = route_ref[...]
    y = route[:, TOP_K:TOP_K + 1] * _from_tiles(gbuf[slot, 0])
    for k in range(1, TOP_K):
        y = y + route[:, TOP_K + k:TOP_K + k + 1] * _from_tiles(gbuf[slot, k])
    h = DEEPNORM_ALPHA * h1_ref[...] + _per_batch(y, g2_ref[...])
    h2 = _layer_norm(h) * lng_ref[...] + lnb_ref[...]
    o_ref[...] = jnp.swapaxes(h2.reshape(TOK_TILE // BATCH, BATCH, D), 0, 1)


def _combine(dest, h1, route, g2, ln2g, ln2b, yb):
    row_spec = pl.BlockSpec((TOK_TILE, D), lambda i, *_: (i, 0))
    vec_spec = pl.BlockSpec((1, D), lambda i, *_: (0, 0))
    return pl.pallas_call(
        _combine_kernel,
        out_shape=jax.ShapeDtypeStruct((BATCH, SEQ, D), F32),
        grid_spec=pltpu.PrefetchScalarGridSpec(
            num_scalar_prefetch=1,
            grid=(T // TOK_TILE,),
            in_specs=[row_spec,
                      pl.BlockSpec((TOK_TILE, ROUTE_W), lambda i, *_: (i, 0)),
                      pl.BlockSpec((BATCH, D), lambda i, *_: (0, 0)),
                      vec_spec, vec_spec,
                      pl.BlockSpec(memory_space=pl.ANY)],
            out_specs=pl.BlockSpec((BATCH, TOK_TILE // BATCH, D), lambda i, *_: (0, i, 0)),
            scratch_shapes=[pltpu.VMEM((2, TOP_K, TOK_TILE, SUBLANES, LANES), F32),
                            pltpu.SemaphoreType.DMA((2,))],
        ),
        compiler_params=pltpu.CompilerParams(dimension_semantics=("arbitrary",),
                                             vmem_limit_bytes=32 * 1024 * 1024),
        name="moe_combine",
    )(dest, h1, route, g2, ln2g, ln2b, yb)


def _routing_plan(route, counts):
    top_idx = route[:, :TOP_K].astype(jnp.int32)
    slot = route[:, 2 * TOP_K:3 * TOP_K].astype(jnp.int32)
    counts = counts[0, :N_EXPERTS].astype(jnp.int32)
    padded = (counts + EBLK - 1) // EBLK * EBLK
    pend = jnp.cumsum(padded)
    pstart = pend - padded
    onehot = top_idx[:, :, None] == jnp.arange(N_EXPERTS, dtype=jnp.int32)
    dest = jnp.sum(jnp.where(onehot, pstart, 0), axis=-1) + slot
    n_active = (pend[-1] // EBLK).astype(jnp.int32).reshape(1)
    blk_start = jnp.arange(N_EBLK, dtype=jnp.int32) * EBLK
    block_e = jnp.minimum((pend[None, :] <= blk_start[:, None]).astype(jnp.int32).sum(1), N_EXPERTS - 1)
    zrow = jnp.maximum(pend - EBLK, 0).astype(jnp.int32)
    zflag = (padded > 0).astype(jnp.int32)
    return dest.reshape(-1).astype(jnp.int32), block_e.astype(jnp.int32), n_active, zrow, zflag


def kernel(x, c, w_ada, b_ada, w_in, b_in, w_cv_dw, b_cv_dw, ln_cv_g, ln_cv_b, w_cv_out, w_lru_conv, b_lru_conv, w_lru_a, b_lru_a, w_lru_x, b_lru_x, lru_lambda, w_lru_out, w_o, b_o, ln1_g, ln1_b, w_router, b_router, w_up, b_up, w_down, b_down, ln2_g, ln2_b):
    def vec(v):
        return v.reshape(1, -1)

    mod = _modulation(c, w_ada[0], b_ada[0])

    w_r = jnp.pad(w_router[0], ((0, 0), (0, ROUTE_W - N_EXPERTS)))
    w_r_hi = w_r.astype(BF16)
    w_r = jnp.concatenate([w_r_hi, (w_r - w_r_hi.astype(F32)).astype(BF16)], axis=1)
    b_r = jnp.pad(b_router[0], (0, ROUTE_W - N_EXPERTS), constant_values=NEG_BIG)
    h1, u2, route, counts = _mixer(
        x, mod, w_in[0].astype(BF16), vec(b_in[0]), w_cv_dw[0], vec(b_cv_dw[0]), vec(ln_cv_g[0]),
        vec(ln_cv_b[0]), w_cv_out[0].astype(BF16), w_lru_conv[0], vec(b_lru_conv[0]),
        w_lru_a[0].astype(BF16), vec(b_lru_a[0]), w_lru_x[0].astype(BF16), vec(b_lru_x[0]),
        vec(lru_lambda[0]), w_lru_out[0].astype(BF16), w_o[0].astype(BF16), vec(b_o[0]),
        vec(ln1_g[0]), vec(ln1_b[0]), w_r, vec(b_r))

    dest, block_e, n_active, zrow, zflag = _routing_plan(route, counts)
    xb = _dispatch(dest, zrow, zflag, n_active, u2)

    bg = b_up[0][:, 0::2].reshape(N_EXPERTS, 1, D_EXPERT)
    bl = b_up[0][:, 1::2].reshape(N_EXPERTS, 1, D_EXPERT)
    bd = b_down[0].reshape(N_EXPERTS, 1, D)
    yb = _experts(block_e, n_active, xb, w_up[0], bg, bl, w_down[0], bd)

    return _combine(dest, h1, route, mod[:, 5 * D:], vec(ln2_g[0]), vec(ln2_b[0]), yb)
```

```python
import functools

import jax
import jax.numpy as jnp
from jax import lax
from jax.experimental import pallas as pl
from jax.experimental.pallas import tpu as pltpu

F32 = jnp.float32
BF16 = jnp.bfloat16
HIGHEST = lax.Precision.HIGHEST

D = 1024
BATCH = 8
SEQ = 2048
T = BATCH * SEQ
CONV_K = 31
LRU_CONV_K = 4
LRU_HEADS = 8
LRU_HEAD_DIM = D // LRU_HEADS
LRU_C = 8.0
N_EXPERTS = 32
TOP_K = 4
D_EXPERT = D
SWIGLU_LIMIT = 7.0
SWIGLU_ALPHA = 1.702
LN_EPS = 1e-5
DEEPNORM_ALPHA = 2.0 ** 0.25
IN_COLS = 6 * D

LANES = 128
SUBLANES = 8
MXU_N = 256
assert D == SUBLANES * LANES

TS = 64
ROWS = TS * BATCH
CV_HIST = (CONV_K - 1) * BATCH
LRU_HIST = (LRU_CONV_K - 1) * BATCH
CONV_CHUNK = 64

EBLK = 512
N_EBLK = (T * TOP_K + N_EXPERTS * (EBLK - 1) + EBLK - 1) // EBLK
N_EROWS = N_EBLK * EBLK
DISP_TILE = 1024
TOK_TILE = 256

ROUTE_W = LANES
NEG_BIG = -1e30


def _layer_norm(x):
    mu = jnp.mean(x, axis=-1, keepdims=True)
    xc = x - mu
    var = jnp.mean(xc * xc, axis=-1, keepdims=True)
    return xc * lax.rsqrt(var + LN_EPS)


def _per_batch(v, scale=None, shift=None):
    rows = v.shape[0]
    v3 = v.reshape(rows // BATCH, BATCH, D)
    if scale is not None:
        v3 = v3 * scale[None]
    if shift is not None:
        v3 = v3 + shift[None]
    return v3.reshape(rows, D)


def _bdot(a, w):
    return jnp.dot(a.astype(BF16), w, preferred_element_type=F32)


def _to_tiles(x):
    rows = x.shape[0]
    cols = [x[:, s * LANES:(s + 1) * LANES].reshape(rows // SUBLANES, SUBLANES, LANES) for s in range(SUBLANES)]
    return jnp.swapaxes(jnp.stack(cols, axis=1), 1, 2).reshape(rows, SUBLANES, LANES)


def _from_tiles(g):
    rows = g.shape[0]
    h = jnp.swapaxes(g.reshape(rows // SUBLANES, SUBLANES, SUBLANES, LANES), 1, 2)
    return jnp.concatenate([h[:, s].reshape(rows, LANES) for s in range(SUBLANES)], axis=-1)


def _zero_after(v):
    return jnp.where(v[0:SUBLANES, 0:LANES] > 0.0, 1.0, 0.0) * 0.0


def _mod_kernel(c_ref, w_ref, b_ref, o_ref):
    c = c_ref[...]
    o_ref[...] = jnp.dot(c * jax.nn.sigmoid(c), w_ref[...], preferred_element_type=F32,
                         precision=HIGHEST) + b_ref[...]


def _modulation(c, w_ada, b_ada):
    tn = 1024
    return pl.pallas_call(
        _mod_kernel,
        out_shape=jax.ShapeDtypeStruct((BATCH, 6 * D), F32),
        grid=(6 * D // tn,),
        in_specs=[pl.BlockSpec((BATCH, D), lambda j: (0, 0)),
                  pl.BlockSpec((D, tn), lambda j: (0, j)),
                  pl.BlockSpec((1, tn), lambda j: (0, j))],
        out_specs=pl.BlockSpec((BATCH, tn), lambda j: (0, j)),
        name="adaln_mod",
    )(c, w_ada, b_ada.reshape(1, 6 * D))


def _mixer_kernel(x_ref, mod_ref, w_in_ref, b_in_ref, w_cvdw_ref, b_cvdw_ref, lncv_g_ref, lncv_b_ref,
                  w_cvout_ref, w_lconv_ref, b_lconv_ref, w_la_ref, b_la_ref, w_lx_ref, b_lx_ref,
                  lam_ref, w_lout_ref, w_o_ref, b_o_ref, ln1g_ref, ln1b_ref, w_r_ref, b_r_ref,
                  h1_ref, u2_ref, route_ref, counts_ref,
                  cvbuf, cvo_s, lbuf, a_s, u_s, hs_s, hstate, cnt_s):
    i = pl.program_id(0)

    @pl.when(i == 0)
    def _():
        cvbuf[0:CV_HIST, :] = jnp.zeros((CV_HIST, D), F32)
        lbuf[0:LRU_HIST, :] = jnp.zeros((LRU_HIST, D), F32)
        hstate[...] = jnp.zeros((BATCH, D), F32)
        cnt_s[...] = jnp.zeros((1, ROUTE_W), F32)

    def mod(j):
        return mod_ref[:, j * D:(j + 1) * D]

    x = jnp.swapaxes(x_ref[...], 0, 1).reshape(ROWS, D)
    u = _per_batch(_layer_norm(x), 1.0 + mod(1), mod(0))
    ub = u.astype(BF16)

    def in_proj(lo, hi):
        return jnp.dot(ub, w_in_ref[:, lo:hi], preferred_element_type=F32) + b_in_ref[:, lo:hi]

    n_pairs = D // MXU_N
    n_chunks = ROWS // CONV_CHUNK
    group = (IN_COLS - 2 * D) // (2 * n_pairs)
    parts = []
    for j in range(n_pairs):
        cols = slice(j * MXU_N, (j + 1) * MXU_N)
        z_val = in_proj(j * MXU_N, (j + 1) * MXU_N)
        z_gate = in_proj(D + j * MXU_N, D + (j + 1) * MXU_N)
        cvbuf[CV_HIST:CV_HIST + ROWS, cols] = z_val * jax.nn.sigmoid(z_gate)
        ties = {}
        for g in range(2):
            lo = 2 * D + (2 * j + g) * group
            parts.append(in_proj(lo, lo + group))
            ties[g * (n_chunks // 2)] = jnp.tile(_zero_after(parts[-1]), (CONV_CHUNK // SUBLANES, 1))
        for rc in range(n_chunks):
            r0 = rc * CONV_CHUNK
            for c in range(2 * j, 2 * j + 2):
                ls = slice(c * LANES, (c + 1) * LANES)
                acc = jnp.broadcast_to(b_cvdw_ref[:, ls], (CONV_CHUNK, LANES))
                if c == 2 * j and rc in ties:
                    acc = acc + ties[rc]
                for k in range(CONV_K):
                    acc = acc + w_cvdw_ref[k:k + 1, ls] * cvbuf[r0 + BATCH * k:r0 + BATCH * k + CONV_CHUNK, ls]
                cvo_s[r0:r0 + CONV_CHUNK, ls] = acc
    z_rest = jnp.concatenate(parts, axis=-1)
    lbuf[LRU_HIST:LRU_HIST + ROWS, :] = z_rest[:, :D]
    zg = z_rest[:, D:2 * D]
    zm = z_rest[:, 2 * D:]
    cvbuf[0:CV_HIST, :] = cvbuf[ROWS:ROWS + CV_HIST, :]

    ya = _layer_norm(cvo_s[...]) * lncv_g_ref[...] + lncv_b_ref[...]
    ya = ya * jax.nn.sigmoid(ya)
    ya = _bdot(ya, w_cvout_ref[...])

    xl = jnp.broadcast_to(b_lconv_ref[...], (ROWS, D))
    for k in range(LRU_CONV_K):
        xl = xl + w_lconv_ref[k:k + 1, :] * lbuf[BATCH * k:BATCH * k + ROWS, :]
    lbuf[0:LRU_HIST, :] = lbuf[ROWS:ROWS + LRU_HIST, :]

    ga, gx = [], []
    for h in range(LRU_HEADS):
        xh = xl[:, h * LRU_HEAD_DIM:(h + 1) * LRU_HEAD_DIM].astype(BF16)
        ga.append(jnp.dot(xh, w_la_ref[h], preferred_element_type=F32))
        gx.append(jnp.dot(xh, w_lx_ref[h], preferred_element_type=F32))
    gate_a = jax.nn.sigmoid(jnp.concatenate(ga, axis=-1) + b_la_ref[...])
    gate_x = jax.nn.sigmoid(jnp.concatenate(gx, axis=-1) + b_lx_ref[...])
    nlam = -lam_ref[...]
    softplus = jnp.maximum(nlam, 0.0) + jnp.log1p(jnp.exp(-jnp.abs(nlam)))
    log_a = (-LRU_C * gate_a) * softplus
    a = jnp.exp(log_a)
    mult = jnp.sqrt(1.0 - a * a)
    row = lax.broadcasted_iota(jnp.int32, (ROWS, D), 0) + i * ROWS
    mult = jnp.where(row < BATCH, 1.0, mult)
    a_s[...] = a
    u_s[...] = mult * (gate_x * xl)

    tie_ya = jnp.concatenate([_zero_after(ya), jnp.zeros((BATCH, D - LANES), F32)], axis=1)
    h = hstate[...] + tie_ya
    for t in range(TS):
        h = a_s[t * BATCH:(t + 1) * BATCH, :] * h + u_s[t * BATCH:(t + 1) * BATCH, :]
        hs_s[t * BATCH:(t + 1) * BATCH, :] = h
    hstate[...] = h

    gelu = 0.5 * zg * (1.0 + jnp.tanh(0.7978845608028654 * (zg + 0.044715 * (zg * zg * zg))))
    yb = _bdot(hs_s[...] * gelu, w_lout_ref[...])

    gm = jax.nn.sigmoid(zm)
    mix = _bdot(gm[:, :D] * ya + gm[:, D:] * yb, w_o_ref[...]) + b_o_ref[...]
    h1 = _layer_norm(DEEPNORM_ALPHA * x + _per_batch(mix, mod(2))) * ln1g_ref[...] + ln1b_ref[...]
    h1_ref[...] = h1

    u2 = _per_batch(_layer_norm(h1), 1.0 + mod(4), mod(3))
    u2_ref[...] = _to_tiles(u2)
    u2_hi = u2.astype(BF16)
    u2_lo = (u2 - u2_hi.astype(F32)).astype(BF16)
    p_hi = jnp.dot(u2_hi, w_r_ref[...], preferred_element_type=F32)
    p_lo = jnp.dot(u2_lo, w_r_ref[...], preferred_element_type=F32)
    logits = (p_hi[:, :ROUTE_W] + (p_hi[:, ROUTE_W:] + p_lo[:, :ROUTE_W] + p_lo[:, ROUTE_W:])) + b_r_ref[...]
    lane = lax.broadcasted_iota(jnp.int32, (ROWS, ROUTE_W), 1)
    vals, idxs = [], []
    for _ in range(TOP_K):
        m = jnp.max(logits, axis=-1, keepdims=True)
        idx = jnp.min(jnp.where(logits == m, lane, ROUTE_W), axis=-1, keepdims=True)
        vals.append(m)
        idxs.append(idx)
        logits = jnp.where(lane == idx, -jnp.inf, logits)
    exps = [jnp.exp(v - vals[0]) for v in vals]
    denom = exps[0] + exps[1] + exps[2] + exps[3]
    route = jnp.zeros((ROWS, ROUTE_W), F32)
    for k in range(TOP_K):
        route = jnp.where(lane == k, idxs[k].astype(F32), route)
        route = jnp.where(lane == TOP_K + k, exps[k] / denom, route)

    sel = jnp.zeros((ROWS, ROUTE_W), F32)
    for k in range(TOP_K):
        sel = jnp.where(lane == idxs[k], 1.0, sel)
    tri = (lax.broadcasted_iota(jnp.int32, (ROWS, ROWS), 0)
           >= lax.broadcasted_iota(jnp.int32, (ROWS, ROWS), 1)).astype(BF16)
    seen = jnp.dot(tri, sel.astype(BF16), preferred_element_type=F32) + cnt_s[...]
    for k in range(TOP_K):
        slot = jnp.sum(jnp.where(lane == idxs[k], seen - 1.0, 0.0), axis=-1, keepdims=True)
        route = jnp.where(lane == 2 * TOP_K + k, slot, route)
    route_ref[...] = route
    cnt_s[...] = seen[ROWS - 1:ROWS, :]
    counts_ref[...] = jnp.broadcast_to(seen[ROWS - 1:ROWS, :], (SUBLANES, ROUTE_W))


def _const_spec(shape):
    nd = len(shape)
    return pl.BlockSpec(shape, lambda i: (0,) * nd, pipeline_mode=pl.Buffered(1))


def _mixer(xt, mod, w_in, b_in, w_cvdw, b_cvdw, lncv_g, lncv_b, w_cvout, w_lconv, b_lconv,
           w_la, b_la, w_lx, b_lx, lam, w_lout, w_o, b_o, ln1g, ln1b, w_r, b_r):
    consts = (mod, w_in, b_in, w_cvdw, b_cvdw, lncv_g, lncv_b, w_cvout, w_lconv, b_lconv,
              w_la, b_la, w_lx, b_lx, lam, w_lout, w_o, b_o, ln1g, ln1b, w_r, b_r)
    row_spec = pl.BlockSpec((ROWS, D), lambda i: (i, 0))
    return pl.pallas_call(
        _mixer_kernel,
        out_shape=(jax.ShapeDtypeStruct((T, D), F32), jax.ShapeDtypeStruct((T, SUBLANES, LANES), F32),
                   jax.ShapeDtypeStruct((T, ROUTE_W), F32),
                   jax.ShapeDtypeStruct((SUBLANES, ROUTE_W), F32)),
        grid=(SEQ // TS,),
        in_specs=[pl.BlockSpec((BATCH, TS, D), lambda i: (0, i, 0))] + [_const_spec(a.shape) for a in consts],
        out_specs=(row_spec, pl.BlockSpec((ROWS, SUBLANES, LANES), lambda i: (i, 0, 0)),
                   pl.BlockSpec((ROWS, ROUTE_W), lambda i: (i, 0)),
                   pl.BlockSpec((SUBLANES, ROUTE_W), lambda i: (0, 0))),
        scratch_shapes=[pltpu.VMEM((CV_HIST + ROWS, D), F32),
                        pltpu.VMEM((ROWS, D), F32),
                        pltpu.VMEM((LRU_HIST + ROWS, D), F32),
                        pltpu.VMEM((ROWS, D), F32),
                        pltpu.VMEM((ROWS, D), F32),
                        pltpu.VMEM((ROWS, D), F32),
                        pltpu.VMEM((BATCH, D), F32),
                        pltpu.VMEM((1, ROUTE_W), F32)],
        compiler_params=pltpu.CompilerParams(dimension_semantics=("arbitrary",),
                                             vmem_limit_bytes=58 * 1024 * 1024),
        name="mixer_router",
    )(xt, *consts)


def _dispatch_kernel(dest_ref, zrow_ref, zflag_ref, na_ref, u_ref, xb_hbm, zbuf, zsem, sem):
    i = pl.program_id(0)

    def zero_block(row):
        return pltpu.make_async_copy(zbuf, xb_hbm.at[pl.ds(pl.multiple_of(row, EBLK), EBLK)], zsem)

    @pl.when(i == 0)
    def _():
        zbuf[...] = jnp.zeros((EBLK, SUBLANES, LANES), F32)
        for e in range(N_EXPERTS):
            @pl.when(zflag_ref[e] > 0)
            def _():
                zero_block(zrow_ref[e]).start()

        def tail_start(b, carry):
            zero_block(b * EBLK).start()
            return carry

        def tail_wait(b, carry):
            zero_block(b * EBLK).wait()
            return carry

        lax.fori_loop(na_ref[0], N_EBLK, tail_start, 0)
        for e in range(N_EXPERTS):
            @pl.when(zflag_ref[e] > 0)
            def _():
                zero_block(zrow_ref[e]).wait()
        lax.fori_loop(na_ref[0], N_EBLK, tail_wait, 0)

    slot = i % 2

    def issue(r, carry):
        tok = i * DISP_TILE + r
        for k in range(TOP_K):
            d = dest_ref[tok * TOP_K + k]
            pltpu.make_async_copy(u_ref.at[tok], xb_hbm.at[d], sem.at[slot]).start(priority=k % 2)
        return carry

    def wait_step(buf):
        for k in range(TOP_K):
            pltpu.make_async_copy(u_ref.at[pl.ds(0, DISP_TILE)], xb_hbm.at[pl.ds(0, DISP_TILE)],
                                  sem.at[buf]).wait()

    lax.fori_loop(0, DISP_TILE, issue, 0, unroll=8)

    @pl.when(i >= 1)
    def _():
        wait_step(1 - slot)

    @pl.when(i == pl.num_programs(0) - 1)
    def _():
        wait_step(slot)


def _dispatch(dest, zrow, zflag, n_active, u2):
    return pl.pallas_call(
        _dispatch_kernel,
        out_shape=jax.ShapeDtypeStruct((N_EROWS, SUBLANES, LANES), F32),
        grid_spec=pltpu.PrefetchScalarGridSpec(
            num_scalar_prefetch=4,
            grid=(T // DISP_TILE,),
            in_specs=[pl.BlockSpec(memory_space=pl.ANY)],
            out_specs=pl.BlockSpec(memory_space=pl.ANY),
            scratch_shapes=[pltpu.VMEM((EBLK, SUBLANES, LANES), F32), pltpu.SemaphoreType.DMA,
                            pltpu.SemaphoreType.DMA((2,))],
        ),
        compiler_params=pltpu.CompilerParams(dimension_semantics=("arbitrary",),
                                             vmem_limit_bytes=24 * 1024 * 1024),
        name="moe_dispatch",
    )(dest, zrow, zflag, n_active, u2)


def _expert_kernel(be_ref, na_ref, x_ref, wup_ref, bg_ref, bl_ref, wdn_ref, bd_ref, perm_ref, o_ref,
                   wg_s, wl_s, wd_s):
    b = pl.program_id(0)
    active = b < na_ref[0]
    new_expert = jnp.logical_or(b == 0, be_ref[b] != be_ref[jnp.maximum(b - 1, 0)])

    @pl.when(jnp.logical_and(active, new_expert))
    def _():
        for s in range(2 * D_EXPERT // MXU_N):
            slab = wup_ref[0, :, s * MXU_N:(s + 1) * MXU_N].astype(BF16)
            split = jnp.dot(slab, perm_ref[...], preferred_element_type=F32).astype(BF16)
            half = MXU_N // 2
            wg_s[:, s * half:(s + 1) * half] = split[:, :half]
            wl_s[:, s * half:(s + 1) * half] = split[:, half:]
        wd_s[...] = wdn_ref[0].astype(BF16)

    @pl.when(active)
    def _():
        xb = _from_tiles(x_ref[...]).astype(BF16)
        zg = jnp.dot(xb, wg_s[...], preferred_element_type=F32) + bg_ref[0]
        zl = jnp.dot(xb, wl_s[...], preferred_element_type=F32) + bl_ref[0]
        zg = jnp.minimum(zg, SWIGLU_LIMIT)
        zl = jnp.clip(zl, -SWIGLU_LIMIT, SWIGLU_LIMIT)
        act = zg * jax.nn.sigmoid(SWIGLU_ALPHA * zg) * (zl + 1.0)
        y = jnp.dot(act.astype(BF16), wd_s[...], preferred_element_type=F32) + bd_ref[0]
        o_ref[...] = _to_tiles(y)

    @pl.when(jnp.logical_not(active))
    def _():
        o_ref[...] = jnp.zeros((EBLK, SUBLANES, LANES), F32)


def _experts(block_e, n_active, xb, w_up, bg, bl, w_down, bd):
    def row_map(b, be, na):
        return (jnp.maximum(jnp.minimum(b, na[0] - 1), 0), 0, 0)

    def w_map(b, be, na):
        return (be[b], 0, 0)

    src = jnp.arange(MXU_N, dtype=jnp.int32)[:, None]
    dst = jnp.arange(MXU_N, dtype=jnp.int32)[None, :]
    half = MXU_N // 2
    perm = jnp.where(dst < half, src == 2 * dst, src == 2 * (dst - half) + 1).astype(BF16)

    return pl.pallas_call(
        _expert_kernel,
        out_shape=jax.ShapeDtypeStruct((N_EROWS, SUBLANES, LANES), F32),
        grid_spec=pltpu.PrefetchScalarGridSpec(
            num_scalar_prefetch=2,
            grid=(N_EBLK,),
            in_specs=[pl.BlockSpec((EBLK, SUBLANES, LANES), row_map),
                      pl.BlockSpec((1, D, 2 * D_EXPERT), w_map),
                      pl.BlockSpec((1, 1, D_EXPERT), w_map),
                      pl.BlockSpec((1, 1, D_EXPERT), w_map),
                      pl.BlockSpec((1, D_EXPERT, D), w_map),
                      pl.BlockSpec((1, 1, D), w_map),
                      pl.BlockSpec((MXU_N, MXU_N), lambda b, be, na: (0, 0))],
            out_specs=pl.BlockSpec((EBLK, SUBLANES, LANES), lambda b, be, na: (b, 0, 0)),
            scratch_shapes=[pltpu.VMEM((D, D_EXPERT), BF16), pltpu.VMEM((D, D_EXPERT), BF16),
                            pltpu.VMEM((D_EXPERT, D), BF16)],
        ),
        compiler_params=pltpu.CompilerParams(dimension_semantics=("arbitrary",),
                                             vmem_limit_bytes=52 * 1024 * 1024),
        name="moe_experts",
    )(block_e, n_active, xb, w_up, bg, bl, w_down, bd, perm)


def _combine_kernel(dest_ref, h1_ref, route_ref, g2_ref, lng_ref, lnb_ref, yb_hbm, o_ref, gbuf, sem):
    i = pl.program_id(0)
    slot = i % 2

    def issue(tile, buf):
        def body(r, carry):
            for k in range(TOP_K):
                d = dest_ref[(tile * TOK_TILE + r) * TOP_K + k]
                pltpu.make_async_copy(yb_hbm.at[d], gbuf.at[buf, k, r], sem.at[buf]).start(priority=k % 2)
            return carry

        lax.fori_loop(0, TOK_TILE, body, 0, unroll=8)

    @pl.when(i == 0)
    def _():
        issue(0, 0)

    for k in range(TOP_K):
        pltpu.make_async_copy(yb_hbm.at[pl.ds(0, TOK_TILE)], gbuf.at[slot, k], sem.at[slot]).wait()

    @pl.when(i + 1 < pl.num_programs(0))
    def _():
        issue(i + 1, 1 - slot)

    route = route_ref[...]
    y = route[:, TOP_K:TOP_K + 1] * _from_tiles(gbuf[slot, 0])
    for k in range(1, TOP_K):
        y = y + route[:, TOP_K + k:TOP_K + k + 1] * _from_tiles(gbuf[slot, k])
    h = DEEPNORM_ALPHA * h1_ref[...] + _per_batch(y, g2_ref[...])
    h2 = _layer_norm(h) * lng_ref[...] + lnb_ref[...]
    o_ref[...] = jnp.swapaxes(h2.reshape(TOK_TILE // BATCH, BATCH, D), 0, 1)


def _combine(dest, h1, route, g2, ln2g, ln2b, yb):
    row_spec = pl.BlockSpec((TOK_TILE, D), lambda i, *_: (i, 0))
    vec_spec = pl.BlockSpec((1, D), lambda i, *_: (0, 0))
    return pl.pallas_call(
        _combine_kernel,
        out_shape=jax.ShapeDtypeStruct((BATCH, SEQ, D), F32),
        grid_spec=pltpu.PrefetchScalarGridSpec(
            num_scalar_prefetch=1,
            grid=(T // TOK_TILE,),
            in_specs=[row_spec,
                      pl.BlockSpec((TOK_TILE, ROUTE_W), lambda i, *_: (i, 0)),
                      pl.BlockSpec((BATCH, D), lambda i, *_: (0, 0)),
                      vec_spec, vec_spec,
                      pl.BlockSpec(memory_space=pl.ANY)],
            out_specs=pl.BlockSpec((BATCH, TOK_TILE // BATCH, D), lambda i, *_: (0, i, 0)),
            scratch_shapes=[pltpu.VMEM((2, TOP_K, TOK_TILE, SUBLANES, LANES), F32),
                            pltpu.SemaphoreType.DMA((2,))],
        ),
        compiler_params=pltpu.CompilerParams(dimension_semantics=("arbitrary",),
                                             vmem_limit_bytes=32 * 1024 * 1024),
        name="moe_combine",
    )(dest, h1, route, g2, ln2g, ln2b, yb)


def _routing_plan(route, counts):
    top_idx = route[:, :TOP_K].astype(jnp.int32)
    slot = route[:, 2 * TOP_K:3 * TOP_K].astype(jnp.int32)
    counts = counts[0, :N_EXPERTS].astype(jnp.int32)
    padded = (counts + EBLK - 1) // EBLK * EBLK
    pend = jnp.cumsum(padded)
    pstart = pend - padded
    onehot = top_idx[:, :, None] == jnp.arange(N_EXPERTS, dtype=jnp.int32)
    dest = jnp.sum(jnp.where(onehot, pstart, 0), axis=-1) + slot
    n_active = (pend[-1] // EBLK).astype(jnp.int32).reshape(1)
    blk_start = jnp.arange(N_EBLK, dtype=jnp.int32) * EBLK
    block_e = jnp.minimum((pend[None, :] <= blk_start[:, None]).astype(jnp.int32).sum(1), N_EXPERTS - 1)
    zrow = jnp.maximum(pend - EBLK, 0).astype(jnp.int32)
    zflag = (padded > 0).astype(jnp.int32)
    return dest.reshape(-1).astype(jnp.int32), block_e.astype(jnp.int32), n_active, zrow, zflag


def kernel(x, c, w_ada, b_ada, w_in, b_in, w_cv_dw, b_cv_dw, ln_cv_g, ln_cv_b, w_cv_out, w_lru_conv, b_lru_conv, w_lru_a, b_lru_a, w_lru_x, b_lru_x, lru_lambda, w_lru_out, w_o, b_o, ln1_g, ln1_b, w_router, b_router, w_up, b_up, w_down, b_down, ln2_g, ln2_b):
    def vec(v):
        return v.reshape(1, -1)

    mod = _modulation(c, w_ada[0], b_ada[0])

    w_r = jnp.pad(w_router[0], ((0, 0), (0, ROUTE_W - N_EXPERTS)))
    w_r_hi = w_r.astype(BF16)
    w_r = jnp.concatenate([w_r_hi, (w_r - w_r_hi.astype(F32)).astype(BF16)], axis=1)
    b_r = jnp.pad(b_router[0], (0, ROUTE_W - N_EXPERTS), constant_values=NEG_BIG)
    h1, u2, route, counts = _mixer(
        x, mod, w_in[0].astype(BF16), vec(b_in[0]), w_cv_dw[0], vec(b_cv_dw[0]), vec(ln_cv_g[0]),
        vec(ln_cv_b[0]), w_cv_out[0].astype(BF16), w_lru_conv[0], vec(b_lru_conv[0]),
        w_lru_a[0].astype(BF16), vec(b_lru_a[0]), w_lru_x[0].astype(BF16), vec(b_lru_x[0]),
        vec(lru_lambda[0]), w_lru_out[0].astype(BF16), w_o[0].astype(BF16), vec(b_o[0]),
        vec(ln1_g[0]), vec(ln1_b[0]), w_r, vec(b_r))

    dest, block_e, n_active, zrow, zflag = _routing_plan(route, counts)
    xb = _dispatch(dest, zrow, zflag, n_active, u2)

    bg = b_up[0][:, 0::2].reshape(N_EXPERTS, 1, D_EXPERT)
    bl = b_up[0][:, 1::2].reshape(N_EXPERTS, 1, D_EXPERT)
    bd = b_down[0].reshape(N_EXPERTS, 1, D)
    yb = _experts(block_e, n_active, xb, w_up[0], bg, bl, w_down[0], bd)

    return _combine(dest, h1, route, mod[:, 5 * D:], vec(ln2_g[0]), vec(ln2_b[0]), yb)
```
